```python
import jax, jax.numpy as jnp
from jax import lax
import numpy as np

D_MODEL = 1024
BATCH = 32
SEQ = 2048
DEPTH = 2

HEAD_DIM = 64
W_FOX = (D_MODEL * 3 // 8) // HEAD_DIM * HEAD_DIM
N_HEADS_FOX = W_FOX // HEAD_DIM
CONV_CH = D_MODEL // 4
W_DIL = D_MODEL - W_FOX - CONV_CH
N_HEADS_DIL = W_DIL // HEAD_DIM
DILATION_PAIRS = ((128, 1), (512, 4), (2048, 16))
CONV_K = 31
FFN_CONV_K = 3
D_FF = ((8 * D_MODEL // 3 + 127) // 128) * 128
Q_BLOCK = 128
EPS = 1e-6
OFF_QA = 0
OFF_KA = OFF_QA + W_FOX
OFF_VA = OFF_KA + W_FOX
OFF_FA = OFF_VA + W_FOX
OFF_QB = OFF_FA + N_HEADS_FOX
OFF_KB = OFF_QB + W_DIL
OFF_VB = OFF_KB + W_DIL
OFF_GV = OFF_VB + W_DIL
OFF_GG = OFF_GV + CONV_CH
N_IN = OFF_GG + CONV_CH

kernel_name = 'hybrid_fox_dilated_conformer_convffn'


def rmsnorm(x, g):
    xf = x.astype(jnp.float32)
    y = xf * lax.rsqrt(jnp.mean(xf * xf, axis=-1, keepdims=True) + EPS)
    return (y * g.astype(jnp.float32)).astype(x.dtype)


def layernorm(x, g, b):
    xf = x.astype(jnp.float32)
    mu = jnp.mean(xf, axis=-1, keepdims=True)
    xc = xf - mu
    y = xc * lax.rsqrt(jnp.mean(xc * xc, axis=-1, keepdims=True) + EPS)
    return (y * g.astype(jnp.float32) + b.astype(jnp.float32)).astype(x.dtype)


def causal_dwconv(x, w, b):
    k, c = w.shape
    y = lax.conv_general_dilated(x, w[:, None, :].astype(x.dtype), window_strides=(1,),
                                 padding=[(k - 1, 0)], dimension_numbers=('NWC', 'WIO', 'NWC'),
                                 feature_group_count=c)
    return y + b.astype(x.dtype)


def forgetting_attention(q, k, v, log_f):
    B, S, H, Dh = q.shape
    nb = S // Q_BLOCK
    c = jnp.cumsum(log_f, axis=1)
    c_k = jnp.transpose(c, (0, 2, 1))[:, :, None, :]
    qb = q.reshape(B, nb, Q_BLOCK, H, Dh).swapaxes(0, 1)
    cb = c.reshape(B, nb, Q_BLOCK, H).swapaxes(0, 1)
    kpos = jnp.arange(S)
    scale = Dh ** -0.5

    def block(args):
        qi, ci, i = args
        s = jnp.einsum('bqhd,bkhd->bhqk', qi, k, preferred_element_type=jnp.float32) * scale
        s = s + jnp.transpose(ci, (0, 2, 1))[..., None] - c_k
        qpos = i * Q_BLOCK + jnp.arange(Q_BLOCK)
        s = jnp.where(kpos[None, :] <= qpos[:, None], s, -jnp.inf)
        p = jax.nn.softmax(s, axis=-1)
        return jnp.einsum('bhqk,bkhd->bqhd', p.astype(v.dtype), v)

    o = lax.map(block, (qb, cb, jnp.arange(nb)))
    return o.swapaxes(0, 1).reshape(B, S, H, Dh)


def dilated_branch(q, k, v, window, dilation):
    B, S, H, Dh = q.shape
    blk = window // dilation
    span = blk * dilation
    sp = -(-S // span) * span
    nb = sp // span
    pad = ((0, 0), (0, sp - S), (0, 0), (0, 0))

    def strided(t):
        return jnp.pad(t, pad).reshape(B, nb, blk, dilation, H, Dh)

    qs, ks, vs = strided(q), strided(k), strided(v)
    def with_prev(t):
        prev = jnp.concatenate([jnp.zeros_like(t[:, :1]), t[:, :-1]], axis=1)
        return jnp.concatenate([prev, t], axis=2)
    kc, vc = with_prev(ks), with_prev(vs)
    s = jnp.einsum('bnqrhd,bnkrhd->bnrhqk', qs, kc, preferred_element_type=jnp.float32) * (Dh ** -0.5)
    qi = blk + jnp.arange(blk)
    ki = jnp.arange(2 * blk)
    delta = qi[:, None] - ki[None, :]
    band = (delta >= 0) & (delta <= blk)
    first = (jnp.arange(nb) == 0)[:, None, None] & (ki < blk)[None, None, :]
    valid = band[None] & ~first
    s = jnp.where(valid[None, :, None, None], s, -jnp.inf)
    m = jnp.max(s, axis=-1, keepdims=True)
    e = jnp.exp(s - m)
    l = jnp.sum(e, axis=-1)
    num = jnp.einsum('bnrhqk,bnkrhd->bnqrhd', e.astype(v.dtype), vc,
                     preferred_element_type=jnp.float32)
    num = num.reshape(B, sp, H, Dh)[:, :S]
    def to_seq(t):
        return jnp.transpose(t, (0, 1, 4, 2, 3)).reshape(B, sp, H)[:, :S]
    return num, to_seq(m[..., 0]), to_seq(l)


def dilated_mixture(q, k, v):
    parts = [dilated_branch(q, k, v, w, d) for (w, d) in DILATION_PAIRS]
    m_all = parts[0][1]
    for _, m, _ in parts[1:]:
        m_all = jnp.maximum(m_all, m)
    num = 0.0
    den = 0.0
    for n_p, m_p, l_p in parts:
        a = jnp.exp(m_p - m_all)
        num = num + a[..., None] * n_p
        den = den + a * l_p
    return (num / den[..., None]).astype(q.dtype)


def setup_inputs(seed: int = 0) -> dict:
    key = jax.random.key(seed)
    ks = jax.random.split(key, 20)
    f32 = jnp.float32
    def nrm(k, shape, scale):
        return jax.random.normal(k, shape, f32) * scale
    return {
        'x': nrm(ks[0], (BATCH, SEQ, D_MODEL), 1.0),
        'ln1_g': 1.0 + nrm(ks[1], (DEPTH, D_MODEL), 0.02),
        'w_in': nrm(ks[2], (DEPTH, D_MODEL, N_IN), D_MODEL ** -0.5),
        'b_forget': jax.random.uniform(ks[3], (DEPTH, N_HEADS_FOX), f32, 1.0, 4.0),
        'g_out_fox': 1.0 + nrm(ks[4], (DEPTH, W_FOX), 0.02),
        'g_out_dil': 1.0 + nrm(ks[5], (DEPTH, W_DIL), 0.02),
        'conv_w': nrm(ks[6], (DEPTH, CONV_K, CONV_CH), CONV_K ** -0.5),
        'conv_b': nrm(ks[7], (DEPTH, CONV_CH), 0.02),
        'cnorm_g': 1.0 + nrm(ks[8], (DEPTH, CONV_CH), 0.02),
        'cnorm_b': nrm(ks[9], (DEPTH, CONV_CH), 0.02),
        'w_o': nrm(ks[10], (DEPTH, D_MODEL, D_MODEL), D_MODEL ** -0.5),
        'ln2_g': 1.0 + nrm(ks[11], (DEPTH, D_MODEL), 0.02),
        'w_up': nrm(ks[12], (DEPTH, D_MODEL, 2 * D_FF), D_MODEL ** -0.5),
        'ffn_conv_w': nrm(ks[13], (DEPTH, FFN_CONV_K, 2 * D_FF), FFN_CONV_K ** -0.5),
        'ffn_conv_b': nrm(ks[14], (DEPTH, 2 * D_FF), 0.02),
        'w_down': nrm(ks[15], (DEPTH, D_FF, D_MODEL), D_FF ** -0.5),
        'g_final': 1.0 + nrm(ks[16], (D_MODEL,), 0.02),
    }


def reference(x, ln1_g, w_in, b_forget, g_out_fox, g_out_dil, conv_w, conv_b, cnorm_g, cnorm_b,
              w_o, ln2_g, w_up, ffn_conv_w, ffn_conv_b, w_down, g_final):
    B, S, _ = x.shape
    for l in range(DEPTH):
        h = rmsnorm(x, ln1_g[l])
        p = jnp.einsum('bsd,dn->bsn', h, w_in[l])
        qa = p[..., OFF_QA:OFF_KA].reshape(B, S, N_HEADS_FOX, HEAD_DIM)
        ka = p[..., OFF_KA:OFF_VA].reshape(B, S, N_HEADS_FOX, HEAD_DIM)
        va = p[..., OFF_VA:OFF_FA].reshape(B, S, N_HEADS_FOX, HEAD_DIM)
        log_f = jax.nn.log_sigmoid(p[..., OFF_FA:OFF_QB].astype(jnp.float32)
                                   + b_forget[l].astype(jnp.float32))
        ya = forgetting_attention(qa, ka, va, log_f).reshape(B, S, W_FOX)
        ya = rmsnorm(ya, g_out_fox[l])
        qb = p[..., OFF_QB:OFF_KB].reshape(B, S, N_HEADS_DIL, HEAD_DIM)
        kb = p[..., OFF_KB:OFF_VB].reshape(B, S, N_HEADS_DIL, HEAD_DIM)
        vb = p[..., OFF_VB:OFF_GV].reshape(B, S, N_HEADS_DIL, HEAD_DIM)
        yb = dilated_mixture(qb, kb, vb).reshape(B, S, W_DIL)
        yb = rmsnorm(yb, g_out_dil[l])
        yc = p[..., OFF_GV:OFF_GG] * jax.nn.sigmoid(p[..., OFF_GG:N_IN])
        yc = causal_dwconv(yc, conv_w[l], conv_b[l])
        yc = jax.nn.silu(layernorm(yc, cnorm_g[l], cnorm_b[l]))
        y = jnp.concatenate([ya, yb, yc], axis=-1)
        x = x + jnp.einsum('bsd,de->bse', y, w_o[l])
        h2 = rmsnorm(x, ln2_g[l])
        u = jnp.einsum('bsd,df->bsf', h2, w_up[l])
        u = causal_dwconv(u, ffn_conv_w[l], ffn_conv_b[l])
        hidden = jax.nn.silu(u[..., :D_FF]) * u[..., D_FF:]
        x = x + jnp.einsum('bsf,fd->bsd', hidden, w_down[l])
    return rmsnorm(x, g_final)
```

```python
import functools

import numpy as np
import jax
import jax.numpy as jnp
from jax import lax
from jax.experimental import pallas as pl
from jax.experimental.pallas import tpu as pltpu

F32 = jnp.float32
BF16 = jnp.bfloat16

HEAD_DIM = 64
LANES = 128
SUBLANES = 8
DILATION_PAIRS = ((128, 1), (512, 4), (2048, 16))
CONV_K = 31
FFN_CONV_K = 3
EPS = 1e-6
VMEM_LIMIT = 56 * 1024 * 1024

TM_INPROJ = 512
TQ = 256
TS_OUT = 512
CONV_HALO = 32
CONV_ROWS = 64
TS_FFN = 512
FFN_HALO = SUBLANES
TF = 256


def _params(*sem):
    return pltpu.CompilerParams(dimension_semantics=sem, vmem_limit_bytes=VMEM_LIMIT)


def _sigmoid(x):
    return 1.0 / (1.0 + jnp.exp(-x))


def _inproj_kernel(x_ref, g_ref, w_ref, qkv_ref, gl_ref, f_ref, h_scr, *, n_att, conv_ch):
    x = x_ref[...]
    ms = jnp.mean(x * x, axis=-1, keepdims=True)
    h_scr[...] = (x * lax.rsqrt(ms + EPS) * g_ref[...]).astype(BF16)
    h = h_scr[...]
    w_att = n_att * LANES
    for t in range(6):
        r = jnp.dot(h, w_ref[:, t * w_att:(t + 1) * w_att], preferred_element_type=F32)
        if t % 3 == 0:
            r = r * (HEAD_DIM ** -0.5)
        for i in range(n_att):
            qkv_ref[t * n_att + i] = r[:, i * LANES:(i + 1) * LANES].astype(BF16)
    off = 6 * w_att
    gv = jnp.dot(h, w_ref[:, off:off + conv_ch], preferred_element_type=F32)
    gg = jnp.dot(h, w_ref[:, off + conv_ch:off + 2 * conv_ch], preferred_element_type=F32)
    gl_ref[...] = (gv * _sigmoid(gg)).astype(BF16)
    off = off + 2 * conv_ch
    f_ref[...] = jnp.dot(h, w_ref[:, off:off + LANES], preferred_element_type=F32)


def _inproj(x, g, w, *, n_att, conv_ch):
    B, S, D = x.shape
    tm = TM_INPROJ
    n_cols = w.shape[1]
    kern = functools.partial(_inproj_kernel, n_att=n_att, conv_ch=conv_ch)
    return pl.pallas_call(
        kern,
        grid=(B, S // tm),
        in_specs=[
            pl.BlockSpec((None, tm, D), lambda b, i: (b, i, 0)),
            pl.BlockSpec((1, D), lambda b, i: (0, 0)),
            pl.BlockSpec((D, n_cols), lambda b, i: (0, 0)),
        ],
        out_specs=[
            pl.BlockSpec((None, 6 * n_att, tm, LANES), lambda b, i: (b, 0, i, 0)),
            pl.BlockSpec((None, tm, conv_ch), lambda b, i: (b, i, 0)),
            pl.BlockSpec((None, tm, LANES), lambda b, i: (b, i, 0)),
        ],
        out_shape=[
            jax.ShapeDtypeStruct((B, 6 * n_att, S, LANES), BF16),
            jax.ShapeDtypeStruct((B, S, conv_ch), BF16),
            jax.ShapeDtypeStruct((B, S, LANES), F32),
        ],
        scratch_shapes=[pltpu.VMEM((tm, D), BF16)],
        compiler_params=_params("parallel", "parallel"),
        name="inproj",
    )(x, g, w)


def _cumsum_kernel(f_ref, b_ref, ccol_ref, crow_ref):
    S = f_ref.shape[0]
    ri = lax.broadcasted_iota(jnp.int32, (LANES, LANES), 0)
    ci = lax.broadcasted_iota(jnp.int32, (LANES, LANES), 1)
    tri = jnp.where(ci <= ri, 1.0, 0.0).astype(BF16)
    carry = jnp.zeros((1, LANES), F32)
    for blk in range(S // LANES):
        rows = slice(blk * LANES, (blk + 1) * LANES)
        z = f_ref[rows, :] + b_ref[...]
        a = jnp.minimum(z, 0.0) - jnp.log(1.0 + jnp.exp(-jnp.abs(z)))
        hi = a.astype(BF16)
        r1 = a - hi.astype(F32)
        mid = r1.astype(BF16)
        lo = (r1 - mid.astype(F32)).astype(BF16)
        c = (jnp.dot(tri, hi, preferred_element_type=F32)
             + jnp.dot(tri, mid, preferred_element_type=F32)
             + jnp.dot(tri, lo, preferred_element_type=F32)) + carry
        ccol_ref[rows, :] = c
        crow_ref[:, rows] = jnp.transpose(c)[0:SUBLANES, :]
        carry = c[LANES - 1:LANES, :]


def _cumsum(f, bias):
    B, S, _ = f.shape
    return pl.pallas_call(
        _cumsum_kernel,
        grid=(B,),
        in_specs=[
            pl.BlockSpec((None, S, LANES), lambda b: (b, 0, 0)),
            pl.BlockSpec((1, LANES), lambda b: (0, 0)),
        ],
        out_specs=[
            pl.BlockSpec((None, S, LANES), lambda b: (b, 0, 0)),
            pl.BlockSpec((None, SUBLANES, S), lambda b: (b, 0, 0)),
        ],
        out_shape=[
            jax.ShapeDtypeStruct((B, S, LANES), F32),
            jax.ShapeDtypeStruct((B, SUBLANES, S), F32),
        ],
        compiler_params=_params("parallel"),
        name="cumsum",
    )(f, bias)


def _flash_pair(q_ref, k_ref, v_ref, qi, bias_fn, tq):
    lane = lax.broadcasted_iota(jnp.int32, (tq, LANES), 1)
    qf = q_ref[...].astype(F32)
    qm = [jnp.where(lane < HEAD_DIM, qf, 0.0).astype(BF16),
          jnp.where(lane >= HEAD_DIM, qf, 0.0).astype(BF16)]

    def scores(j, kb, diag):
        k0 = pl.multiple_of(kb * tq, tq)
        s = lax.dot_general(qm[j], k_ref[pl.ds(k0, tq), :], (((1,), (1,)), ((), ())),
                            preferred_element_type=F32)
        return s + bias_fn(j, kb, k0, diag), v_ref[pl.ds(k0, tq), :]

    state = []
    for j in range(2):
        s, vblk = scores(j, qi, True)
        m = jnp.max(s, axis=-1, keepdims=True)
        p = jnp.exp(s - m)
        l = jnp.sum(p, axis=-1, keepdims=True)
        acc = jnp.dot(p.astype(BF16), vblk, preferred_element_type=F32)
        state += [m, l, acc]

    def body(kb, carry):
        new = []
        for j in range(2):
            m, l, acc = carry[3 * j:3 * j + 3]
            s, vblk = scores(j, kb, False)
            m_new = jnp.maximum(m, jnp.max(s, axis=-1, keepdims=True))
            alpha = jnp.exp(m - m_new)
            p = jnp.exp(s - m_new)
            l = alpha * l + jnp.sum(p, axis=-1, keepdims=True)
            acc = alpha * acc + jnp.dot(p.astype(BF16), vblk, preferred_element_type=F32)
            new += [m_new, l, acc]
        return tuple(new)

    state = lax.fori_loop(0, qi, body, tuple(state))
    o0 = state[2] / state[1]
    o1 = state[5] / state[4]
    return jnp.where(lane < HEAD_DIM, o0, o1)


def _fox_kernel(q_ref, k_ref, v_ref, ccol_ref, crow_ref, o_ref, *, tq):
    pair = pl.program_id(1)
    qi = pl.program_id(2)
    lane = lax.broadcasted_iota(jnp.int32, (tq, LANES), 1)
    sub = lax.broadcasted_iota(jnp.int32, (SUBLANES, tq), 0)
    ri = lax.broadcasted_iota(jnp.int32, (tq, tq), 0)
    ci = lax.broadcasted_iota(jnp.int32, (tq, tq), 1)
    ccol_blk = ccol_ref[...]
    c_q = [jnp.sum(jnp.where(lane == 2 * pair + j, ccol_blk, 0.0), axis=-1, keepdims=True)
           for j in range(2)]

    def bias_fn(j, kb, k0, diag):
        rows = crow_ref[:, pl.ds(k0, tq)]
        c_k = jnp.sum(jnp.where(sub == 2 * pair + j, rows, 0.0), axis=0, keepdims=True)
        b = c_q[j] - c_k
        if diag:
            b = jnp.where(ci <= ri, b, -jnp.inf)
        return b

    o_ref[...] = _flash_pair(q_ref, k_ref, v_ref, qi, bias_fn, tq).astype(o_ref.dtype)


def _dil_kernel(q_ref, k_ref, v_ref, lw_ref, o_ref, *, tq):
    qi = pl.program_id(2)

    def bias_fn(j, kb, k0, diag):
        return lw_ref[0] if diag else lw_ref[qi - kb]

    o_ref[...] = _flash_pair(q_ref, k_ref, v_ref, qi, bias_fn, tq).astype(o_ref.dtype)


def _qkv_specs(S, tq, n_att, group):
    base = 3 * n_att * group
    return [
        pl.BlockSpec((None, None, tq, LANES), lambda b, p, qi: (b, base + p, qi, 0)),
        pl.BlockSpec((None, None, S, LANES), lambda b, p, qi: (b, base + n_att + p, 0, 0)),
        pl.BlockSpec((None, None, S, LANES), lambda b, p, qi: (b, base + 2 * n_att + p, 0, 0)),
    ]


def _fox_attention(qkv, ccol, crow, *, n_att):
    B, _, S, _ = qkv.shape
    tq = TQ
    return pl.pallas_call(
        functools.partial(_fox_kernel, tq=tq),
        grid=(B, n_att, S // tq),
        in_specs=_qkv_specs(S, tq, n_att, 0) + [
            pl.BlockSpec((None, tq, LANES), lambda b, p, qi: (b, qi, 0)),
            pl.BlockSpec((None, SUBLANES, S), lambda b, p, qi: (b, 0, 0)),
        ],
        out_specs=pl.BlockSpec((None, tq, LANES), lambda b, p, qi: (b, qi, p)),
        out_shape=jax.ShapeDtypeStruct((B, S, n_att * LANES), BF16),
        compiler_params=_params("parallel", "parallel", "arbitrary"),
        name="fox_attn",
    )(qkv, qkv, qkv, ccol, crow)


def _dil_log_weights(S, tq):
    for window, d in DILATION_PAIRS:
        assert S % ((window // d) * d) == 0, "dilated branches assume no sequence padding"
    nq = S // tq
    r = np.arange(tq)[:, None]
    c = np.arange(tq)[None, :]
    tabs = []
    for off in range(nq):
        delta = off * tq + r - c
        w = np.zeros((tq, tq), np.int64)
        for window, d in DILATION_PAIRS:
            w += (delta >= 0) & (delta % d == 0) & (delta // d <= window // d)
        with np.errstate(divide="ignore"):
            tabs.append(np.log(w.astype(np.float64)))
    return jnp.asarray(np.stack(tabs).astype(np.float32))


def _dil_attention(qkv, logw, *, n_att):
    B, _, S, _ = qkv.shape
    tq = TQ
    nq = S // tq
    return pl.pallas_call(
        functools.partial(_dil_kernel, tq=tq),
        grid=(B, n_att, nq),
        in_specs=_qkv_specs(S, tq, n_att, 1) + [
            pl.BlockSpec((nq, tq, tq), lambda b, p, qi: (0, 0, 0)),
        ],
        out_specs=pl.BlockSpec((None, tq, LANES), lambda b, p, qi: (b, qi, p)),
        out_shape=jax.ShapeDtypeStruct((B, S, n_att * LANES), BF16),
        compiler_params=_params("parallel", "parallel", "arbitrary"),
        name="dil_attn",
    )(qkv, qkv, qkv, logw)


def _outproj_kernel(x_ref, oa_ref, ob_ref, gl_ref, glh_ref, gfox_ref, gdil_ref, cw_ref, cb_ref,
                    cng_ref, cnb_ref, wo_ref, o_ref, ext_scr, y_scr, *, ts, w_att, conv_ch):
    i = pl.program_id(1)
    ext_scr[0:CONV_HALO, :] = jnp.where(i > 0, glh_ref[...].astype(F32), 0.0)
    ext_scr[CONV_HALO:, :] = gl_ref[...].astype(F32)
    first = CONV_HALO - (CONV_K - 1)
    for c in range(ts // CONV_ROWS):
        r0 = c * CONV_ROWS
        acc = jnp.broadcast_to(cb_ref[...], (CONV_ROWS, conv_ch))
        for k in range(CONV_K):
            acc = acc + cw_ref[k:k + 1, :] * ext_scr[r0 + first + k:r0 + first + k + CONV_ROWS, :]
        mu = jnp.mean(acc, axis=-1, keepdims=True)
        xc = acc - mu
        yn = xc * lax.rsqrt(jnp.mean(xc * xc, axis=-1, keepdims=True) + EPS)
        yn = yn * cng_ref[...] + cnb_ref[...]
        y_scr[r0:r0 + CONV_ROWS, 2 * w_att:] = (yn * _sigmoid(yn)).astype(BF16)
    for o_in, g_in, lo in ((oa_ref, gfox_ref, 0), (ob_ref, gdil_ref, w_att)):
        a = o_in[...].astype(F32)
        ms = jnp.mean(a * a, axis=-1, keepdims=True)
        y_scr[:, lo:lo + w_att] = (a * lax.rsqrt(ms + EPS) * g_in[...]).astype(BF16)
    o_ref[...] = x_ref[...] + jnp.dot(y_scr[...], wo_ref[...], preferred_element_type=F32)


def _outproj(x, oa, ob, gl, g_fox, g_dil, cw, cb, cng, cnb, wo):
    B, S, D = x.shape
    ts = TS_OUT
    w_att = oa.shape[-1]
    conv_ch = gl.shape[-1]
    hb = ts // CONV_HALO
    kern = functools.partial(_outproj_kernel, ts=ts, w_att=w_att, conv_ch=conv_ch)
    vec = lambda n: pl.BlockSpec((1, n), lambda b, i: (0, 0))
    return pl.pallas_call(
        kern,
        grid=(B, S // ts),
        in_specs=[
            pl.BlockSpec((None, ts, D), lambda b, i: (b, i, 0)),
            pl.BlockSpec((None, ts, w_att), lambda b, i: (b, i, 0)),
            pl.BlockSpec((None, ts, w_att), lambda b, i: (b, i, 0)),
            pl.BlockSpec((None, ts, conv_ch), lambda b, i: (b, i, 0)),
            pl.BlockSpec((None, CONV_HALO, conv_ch), lambda b, i: (b, jnp.maximum(i * hb - 1, 0), 0)),
            vec(w_att), vec(w_att),
            pl.BlockSpec((CONV_K, conv_ch), lambda b, i: (0, 0)),
            vec(conv_ch), vec(conv_ch), vec(conv_ch),
            pl.BlockSpec((D, D), lambda b, i: (0, 0)),
        ],
        out_specs=pl.BlockSpec((None, ts, D), lambda b, i: (b, i, 0)),
        out_shape=jax.ShapeDtypeStruct((B, S, D), F32),
        scratch_shapes=[pltpu.VMEM((ts + CONV_HALO, conv_ch), F32), pltpu.VMEM((ts, D), BF16)],
        compiler_params=_params("parallel", "parallel"),
        name="outproj",
    )(x, oa, ob, gl, gl, g_fox, g_dil, cw, cb, cng, cnb, wo)


def _ffn_kernel(x_ref, xh_ref, g_ref, wup_ref, cw_ref, cb_ref, wdn_ref, gf_ref, o_ref,
                h_scr, u_scr, acc_scr, *, ts, n_chunks, final):
    i = pl.program_id(1)

    def norm(v):
        return (v * lax.rsqrt(jnp.mean(v * v, axis=-1, keepdims=True) + EPS) * g_ref[...]).astype(BF16)

    h_scr[0:FFN_HALO, :] = norm(jnp.where(i > 0, xh_ref[...], 0.0))
    h_scr[FFN_HALO:, :] = norm(x_ref[...])
    acc_scr[...] = jnp.zeros_like(acc_scr)
    first = FFN_HALO - (FFN_CONV_K - 1)

    def chunk(c, carry):
        c2 = pl.multiple_of(c * (2 * TF), 2 * TF)
        u_scr[...] = jnp.dot(h_scr[...], wup_ref[:, pl.ds(c2, 2 * TF)], preferred_element_type=F32)
        w = cw_ref[:, pl.ds(c2, 2 * TF)]
        uc = cb_ref[:, pl.ds(c2, 2 * TF)]
        for k in range(FFN_CONV_K):
            uc = uc + w[k:k + 1, :] * u_scr[first + k:first + k + ts, :]
        gate = uc[:, :TF]
        hid = (gate * _sigmoid(gate) * uc[:, TF:]).astype(BF16)
        c1 = pl.multiple_of(c * TF, TF)
        acc_scr[...] += jnp.dot(hid, wdn_ref[pl.ds(c1, TF), :], preferred_element_type=F32)
        return carry

    lax.fori_loop(0, n_chunks, chunk, 0)
    out = x_ref[...] + acc_scr[...]
    if final:
        out = out * lax.rsqrt(jnp.mean(out * out, axis=-1, keepdims=True) + EPS) * gf_ref[...]
    o_ref[...] = out


def _ffn(x, g, wup, cw, cb, wdn, g_final, *, final):
    B, S, D = x.shape
    ts = TS_FFN
    d_ff = wdn.shape[0]
    n_chunks = d_ff // TF
    hb = ts // FFN_HALO
    kern = functools.partial(_ffn_kernel, ts=ts, n_chunks=n_chunks, final=final)
    const = lambda shape: pl.BlockSpec(shape, lambda b, i: (0,) * len(shape))
    return pl.pallas_call(
        kern,
        grid=(B, S // ts),
        in_specs=[
            pl.BlockSpec((None, ts, D), lambda b, i: (b, i, 0)),
            pl.BlockSpec((None, FFN_HALO, D), lambda b, i: (b, jnp.maximum(i * hb - 1, 0), 0)),
            const((1, D)),
            const((D, 2 * d_ff)),
            const((FFN_CONV_K, 2 * d_ff)),
            const((1, 2 * d_ff)),
            const((d_ff, D)),
            const((1, D)),
        ],
        out_specs=pl.BlockSpec((None, ts, D), lambda b, i: (b, i, 0)),
        out_shape=jax.ShapeDtypeStruct((B, S, D), F32),
        scratch_shapes=[
            pltpu.VMEM((ts + FFN_HALO, D), BF16),
            pltpu.VMEM((ts + FFN_HALO, 2 * TF), F32),
            pltpu.VMEM((ts, D), F32),
        ],
        compiler_params=_params("parallel", "parallel"),
        name="ffn_final" if final else "ffn",
    )(x, x, g, wup, cw, cb, wdn, g_final)


def _interleave_gate_value(a, d_ff):
    lead = a.shape[:-1]
    gate = a[..., :d_ff].reshape(lead + (d_ff // TF, 1, TF))
    val = a[..., d_ff:].reshape(lead + (d_ff // TF, 1, TF))
    return jnp.concatenate([gate, val], axis=-2).reshape(lead + (2 * d_ff,))


def kernel(x, ln1_g, w_in, b_forget, g_out_fox, g_out_dil, conv_w, conv_b, cnorm_g, cnorm_b,
           w_o, ln2_g, w_up, ffn_conv_w, ffn_conv_b, w_down, g_final):
    B, S, D = x.shape
    depth = w_in.shape[0]
    w_fox = g_out_fox.shape[-1]
    w_dil = g_out_dil.shape[-1]
    conv_ch = conv_w.shape[-1]
    n_heads_fox = b_forget.shape[-1]
    d_ff = w_down.shape[1]
    assert w_fox == w_dil and w_fox % LANES == 0 and n_heads_fox == w_fox // HEAD_DIM
    assert n_heads_fox <= SUBLANES and d_ff % TF == 0 and D % LANES == 0
    assert S % TM_INPROJ == 0 and S % TQ == 0 and S % TS_OUT == 0 and S % TS_FFN == 0
    n_att = w_fox // LANES
    off_fa = 3 * w_fox
    off_qb = off_fa + n_heads_fox
    off_gv = off_qb + 3 * w_dil
    assert w_in.shape[-1] == off_gv + 2 * conv_ch

    logw = _dil_log_weights(S, TQ)
    row = lambda v: v.reshape(1, -1)

    for l in range(depth):
        wl = w_in[l]
        w_f = jnp.pad(wl[:, off_fa:off_qb], ((0, 0), (0, LANES - n_heads_fox)))
        w_cat = jnp.concatenate([wl[:, :off_fa], wl[:, off_qb:], w_f], axis=1).astype(BF16)
        b_f = jnp.pad(b_forget[l], (0, LANES - n_heads_fox)).reshape(1, LANES)

        qkv, gl, f = _inproj(x, row(ln1_g[l]), w_cat, n_att=n_att, conv_ch=conv_ch)
        ccol, crow = _cumsum(f, b_f)
        oa = _fox_attention(qkv, ccol, crow, n_att=n_att)
        ob = _dil_attention(qkv, logw, n_att=n_att)
        x = _outproj(x, oa, ob, gl, row(g_out_fox[l]), row(g_out_dil[l]), conv_w[l], row(conv_b[l]),
                     row(cnorm_g[l]), row(cnorm_b[l]), w_o[l].astype(BF16))
        x = _ffn(x, row(ln2_g[l]),
                 _interleave_gate_value(w_up[l], d_ff).astype(BF16),
                 _interleave_gate_value(ffn_conv_w[l], d_ff),
                 _interleave_gate_value(row(ffn_conv_b[l]), d_ff),
                 w_down[l].astype(BF16), row(g_final), final=(l == depth - 1))
    return x
```

```python
import functools

import numpy as np
import jax
import jax.numpy as jnp
from jax import lax
from jax.experimental import pallas as pl
from jax.experimental.pallas import tpu as pltpu

F32 = jnp.float32
BF16 = jnp.bfloat16

HEAD_DIM = 64
LANES = 128
SUBLANES = 8
DILATION_PAIRS = ((128, 1), (512, 4), (2048, 16))
CONV_K = 31
FFN_CONV_K = 3
EPS = 1e-6
VMEM_LIMIT = 56 * 1024 * 1024

TM_INPROJ = 512
TQ = 256
TS_OUT = 512
CONV_HALO = 32
CONV_ROWS = 64
TS_FFN = 512
FFN_HALO = SUBLANES
TF = 256


def _params(*sem):
    return pltpu.CompilerParams(dimension_semantics=sem, vmem_limit_bytes=VMEM_LIMIT)


def _sigmoid(x):
    return 1.0 / (1.0 + jnp.exp(-x))


def _inproj_kernel(x_ref, g_ref, w_ref, qkv_ref, gl_ref, f_ref, h_scr, *, n_att, conv_ch):
    x = x_ref[...]
    ms = jnp.mean(x * x, axis=-1, keepdims=True)
    h_scr[...] = (x * lax.rsqrt(ms + EPS) * g_ref[...]).astype(BF16)
    h = h_scr[...]
    w_att = n_att * LANES
    for t in range(6):
        r = jnp.dot(h, w_ref[:, t * w_att:(t + 1) * w_att], preferred_element_type=F32)
        if t % 3 == 0:
            r = r * (HEAD_DIM ** -0.5)
        for i in range(n_att):
            qkv_ref[t * n_att + i] = r[:, i * LANES:(i + 1) * LANES].astype(BF16)
    off = 6 * w_att
    gv = jnp.dot(h, w_ref[:, off:off + conv_ch], preferred_element_type=F32)
    gg = jnp.dot(h, w_ref[:, off + conv_ch:off + 2 * conv_ch], preferred_element_type=F32)
    gl_ref[...] = (gv * _sigmoid(gg)).astype(BF16)
    off = off + 2 * conv_ch
    f_ref[...] = jnp.dot(h, w_ref[:, off:off + LANES], preferred_element_type=F32)


def _inproj(x, g, w, *, n_att, conv_ch):
    B, S, D = x.shape
    tm = TM_INPROJ
    n_cols = w.shape[1]
    kern = functools.partial(_inproj_kernel, n_att=n_att, conv_ch=conv_ch)
    return pl.pallas_call(
        kern,
        grid=(B, S // tm),
        in_specs=[
            pl.BlockSpec((None, tm, D), lambda b, i: (b, i, 0)),
            pl.BlockSpec((1, D), lambda b, i: (0, 0)),
            pl.BlockSpec((D, n_cols), lambda b, i: (0, 0)),
        ],
        out_specs=[
            pl.BlockSpec((None, 6 * n_att, tm, LANES), lambda b, i: (b, 0, i, 0)),
            pl.BlockSpec((None, tm, conv_ch), lambda b, i: (b, i, 0)),
            pl.BlockSpec((None, tm, LANES), lambda b, i: (b, i, 0)),
        ],
        out_shape=[
            jax.ShapeDtypeStruct((B, 6 * n_att, S, LANES), BF16),
            jax.ShapeDtypeStruct((B, S, conv_ch), BF16),
            jax.ShapeDtypeStruct((B, S, LANES), F32),
        ],
        scratch_shapes=[pltpu.VMEM((tm, D), BF16)],
        compiler_params=_params("parallel", "parallel"),
        name="inproj",
    )(x, g, w)


def _cumsum_kernel(f_ref, b_ref, ccol_ref, crow_ref):
    S = f_ref.shape[0]
    ri = lax.broadcasted_iota(jnp.int32, (LANES, LANES), 0)
    ci = lax.broadcasted_iota(jnp.int32, (LANES, LANES), 1)
    tri = jnp.where(ci <= ri, 1.0, 0.0).astype(BF16)
    carry = jnp.zeros((1, LANES), F32)
    for blk in range(S // LANES):
        rows = slice(blk * LANES, (blk + 1) * LANES)
        z = f_ref[rows, :] + b_ref[...]
        a = jnp.minimum(z, 0.0) - jnp.log(1.0 + jnp.exp(-jnp.abs(z)))
        hi = a.astype(BF16)
        r1 = a - hi.astype(F32)
        mid = r1.astype(BF16)
        lo = (r1 - mid.astype(F32)).astype(BF16)
        c = (jnp.dot(tri, hi, preferred_element_type=F32)
             + jnp.dot(tri, mid, preferred_element_type=F32)
             + jnp.dot(tri, lo, preferred_element_type=F32)) + carry
        ccol_ref[rows, :] = c
        crow_ref[:, rows] = jnp.transpose(c)[0:SUBLANES, :]
        carry = c[LANES - 1:LANES, :]


def _cumsum(f, bias):
    B, S, _ = f.shape
    return pl.pallas_call(
        _cumsum_kernel,
        grid=(B,),
        in_specs=[
            pl.BlockSpec((None, S, LANES), lambda b: (b, 0, 0)),
            pl.BlockSpec((1, LANES), lambda b: (0, 0)),
        ],
        out_specs=[
            pl.BlockSpec((None, S, LANES), lambda b: (b, 0, 0)),
            pl.BlockSpec((None, SUBLANES, S), lambda b: (b, 0, 0)),
        ],
        out_shape=[
            jax.ShapeDtypeStruct((B, S, LANES), F32),
            jax.ShapeDtypeStruct((B, SUBLANES, S), F32),
        ],
        compiler_params=_params("parallel"),
        name="cumsum",
    )(f, bias)


_NT = (((1,), (1,)), ((), ()))


def _attn_two_pass(q_ref, k_ref, v_ref, o_ref, scr, qi, bias_fn, *, tq, n_pairs):
    qm_scr, s_scr, mx_scr, ls_scr, acc_scr = scr
    halves = tq // LANES
    lane = lax.broadcasted_iota(jnp.int32, (tq, LANES), 1)
    for pr in range(n_pairs):
        qf = q_ref[pr].astype(F32)
        qm_scr[2 * pr] = jnp.where(lane < HEAD_DIM, qf, 0.0).astype(BF16)
        qm_scr[2 * pr + 1] = jnp.where(lane >= HEAD_DIM, qf, 0.0).astype(BF16)

    def score_block(kb, diag):
        k0 = pl.multiple_of(kb * tq, tq)
        for pr in range(n_pairs):
            kblk = k_ref[pr, pl.ds(k0, tq), :]
            for j in range(2):
                h = 2 * pr + j
                s = lax.dot_general(qm_scr[h], kblk, _NT, preferred_element_type=F32)
                cols = [bias_fn(h, c, s[:, c * LANES:(c + 1) * LANES], kb, k0, diag)
                        for c in range(halves)]
                s_scr[h, kb] = jnp.concatenate(cols, axis=1)
                blkmax = functools.reduce(jnp.maximum, cols)
                mx_scr[h] = blkmax if diag else jnp.maximum(mx_scr[h], blkmax)

    score_block(qi, True)

    def score_body(kb, carry):
        score_block(kb, False)
        return carry

    lax.fori_loop(0, qi, score_body, 0)

    for h in range(2 * n_pairs):
        m = jnp.max(mx_scr[h], axis=-1, keepdims=True)
        mx_scr[h] = jnp.broadcast_to(m, (tq, LANES))
        ls_scr[h] = jnp.zeros((tq, LANES), F32)
        acc_scr[h] = jnp.zeros((tq, LANES), F32)

    def value_body(kb, carry):
        k0 = pl.multiple_of(kb * tq, tq)
        for pr in range(n_pairs):
            vblk = v_ref[pr, pl.ds(k0, tq), :]
            for j in range(2):
                h = 2 * pr + j
                m = mx_scr[h]
                s = s_scr[h, kb]
                p = [jnp.exp(s[:, c * LANES:(c + 1) * LANES] - m) for c in range(halves)]
                ls_scr[h] += functools.reduce(jnp.add, p)
                acc_scr[h] += jnp.dot(jnp.concatenate(p, axis=1).astype(BF16), vblk,
                                      preferred_element_type=F32)
        return carry

    lax.fori_loop(0, qi + 1, value_body, 0)

    for pr in range(n_pairs):
        o = []
        for j in range(2):
            h = 2 * pr + j
            o.append(acc_scr[h] / jnp.sum(ls_scr[h], axis=-1, keepdims=True))
        o_ref[:, pr * LANES:(pr + 1) * LANES] = jnp.where(lane < HEAD_DIM, o[0], o[1]).astype(o_ref.dtype)


def _causal_mask(c, s, tq):
    ri = lax.broadcasted_iota(jnp.int32, (tq, LANES), 0)
    ci = lax.broadcasted_iota(jnp.int32, (tq, LANES), 1) + c * LANES
    return jnp.where(ci <= ri, s, -jnp.inf)


def _fox_kernel(q_ref, k_ref, v_ref, ccol_ref, crow_ref, o_ref, cq_scr, *scr, tq, n_pairs):
    qi = pl.program_id(1)
    ccol_blk = ccol_ref[...]
    for h in range(2 * n_pairs):
        cq_scr[h] = jnp.broadcast_to(ccol_blk[:, h:h + 1], (tq, LANES))

    def bias_fn(h, c, s, kb, k0, diag):
        c_k = crow_ref[h:h + 1, pl.ds(k0 + c * LANES, LANES)]
        s = s + (cq_scr[h] - c_k)
        return _causal_mask(c, s, tq) if diag else s

    _attn_two_pass(q_ref, k_ref, v_ref, o_ref, scr, qi, bias_fn, tq=tq, n_pairs=n_pairs)


def _dil_kernel(q_ref, k_ref, v_ref, lw_ref, o_ref, *scr, tq, n_pairs):
    qi = pl.program_id(1)

    def bias_fn(h, c, s, kb, k0, diag):
        off = 0 if diag else qi - kb
        return s + lw_ref[off, :, c * LANES:(c + 1) * LANES]

    _attn_two_pass(q_ref, k_ref, v_ref, o_ref, scr, qi, bias_fn, tq=tq, n_pairs=n_pairs)


def _qkv_specs(S, tq, n_att, group):
    base = 3 * group
    return [
        pl.BlockSpec((None, n_att, tq, LANES), lambda b, qi: (b, base, qi, 0)),
        pl.BlockSpec((None, n_att, S, LANES), lambda b, qi: (b, base + 1, 0, 0)),
        pl.BlockSpec((None, n_att, S, LANES), lambda b, qi: (b, base + 2, 0, 0)),
    ]


def _attn_scratch(S, tq, n_att):
    n_heads = 2 * n_att
    head_tile = pltpu.VMEM((n_heads, tq, LANES), F32)
    return [
        pltpu.VMEM((n_heads, tq, LANES), BF16),
        pltpu.VMEM((n_heads, S // tq, tq, tq), F32),
        head_tile, head_tile, head_tile,
    ]


def _fox_attention(qkv, ccol, crow, *, n_att):
    B, _, S, _ = qkv.shape
    tq = TQ
    return pl.pallas_call(
        functools.partial(_fox_kernel, tq=tq, n_pairs=n_att),
        grid=(B, S // tq),
        in_specs=_qkv_specs(S, tq, n_att, 0) + [
            pl.BlockSpec((None, tq, LANES), lambda b, qi: (b, qi, 0)),
            pl.BlockSpec((None, SUBLANES, S), lambda b, qi: (b, 0, 0)),
        ],
        out_specs=pl.BlockSpec((None, tq, n_att * LANES), lambda b, qi: (b, qi, 0)),
        out_shape=jax.ShapeDtypeStruct((B, S, n_att * LANES), BF16),
        scratch_shapes=[pltpu.VMEM((2 * n_att, tq, LANES), F32)] + _attn_scratch(S, tq, n_att),
        compiler_params=_params("parallel", "arbitrary"),
        name="fox_attn",
    )(qkv, qkv, qkv, ccol, crow)


def _dil_log_weights(S, tq):
    for window, d in DILATION_PAIRS:
        assert S % ((window // d) * d) == 0, "dilated branches assume no sequence padding"
    nq = S // tq
    r = np.arange(tq)[:, None]
    c = np.arange(tq)[None, :]
    tabs = []
    for off in range(nq):
        delta = off * tq + r - c
        w = np.zeros((tq, tq), np.int64)
        for window, d in DILATION_PAIRS:
            w += (delta >= 0) & (delta % d == 0) & (delta // d <= window // d)
        with np.errstate(divide="ignore"):
            tabs.append(np.log(w.astype(np.float64)))
    return jnp.asarray(np.stack(tabs).astype(np.float32))


def _dil_attention(qkv, logw, *, n_att):
    B, _, S, _ = qkv.shape
    tq = TQ
    nq = S // tq
    return pl.pallas_call(
        functools.partial(_dil_kernel, tq=tq, n_pairs=n_att),
        grid=(B, nq),
        in_specs=_qkv_specs(S, tq, n_att, 1) + [
            pl.BlockSpec((nq, tq, tq), lambda b, qi: (0, 0, 0)),
        ],
        out_specs=pl.BlockSpec((None, tq, n_att * LANES), lambda b, qi: (b, qi, 0)),
        out_shape=jax.ShapeDtypeStruct((B, S, n_att * LANES), BF16),
        scratch_shapes=_attn_scratch(S, tq, n_att),
        compiler_params=_params("parallel", "arbitrary"),
        name="dil_attn",
    )(qkv, qkv, qkv, logw)


def _outproj_kernel(x_ref, oa_ref, ob_ref, gl_ref, glh_ref, gfox_ref, gdil_ref, cw_ref, cb_ref,
                    cng_ref, cnb_ref, wo_ref, o_ref, ext_scr, y_scr, *, ts, w_att, conv_ch):
    i = pl.program_id(1)
    ext_scr[0:CONV_HALO, :] = jnp.where(i > 0, glh_ref[...].astype(F32), 0.0)
    ext_scr[CONV_HALO:, :] = gl_ref[...].astype(F32)
    first = CONV_HALO - (CONV_K - 1)
    for c in range(ts // CONV_ROWS):
        r0 = c * CONV_ROWS
        acc = jnp.broadcast_to(cb_ref[...], (CONV_ROWS, conv_ch))
        for k in range(CONV_K):
            acc = acc + cw_ref[k:k + 1, :] * ext_scr[r0 + first + k:r0 + first + k + CONV_ROWS, :]
        mu = jnp.mean(acc, axis=-1, keepdims=True)
        xc = acc - mu
        yn = xc * lax.rsqrt(jnp.mean(xc * xc, axis=-1, keepdims=True) + EPS)
        yn = yn * cng_ref[...] + cnb_ref[...]
        y_scr[r0:r0 + CONV_ROWS, 2 * w_att:] = (yn * _sigmoid(yn)).astype(BF16)
    for o_in, g_in, lo in ((oa_ref, gfox_ref, 0), (ob_ref, gdil_ref, w_att)):
        a = o_in[...].astype(F32)
        ms = jnp.mean(a * a, axis=-1, keepdims=True)
        y_scr[:, lo:lo + w_att] = (a * lax.rsqrt(ms + EPS) * g_in[...]).astype(BF16)
    o_ref[...] = x_ref[...] + jnp.dot(y_scr[...], wo_ref[...], preferred_element_type=F32)


def _outproj(x, oa, ob, gl, g_fox, g_dil, cw, cb, cng, cnb, wo):
    B, S, D = x.shape
    ts = TS_OUT
    w_att = oa.shape[-1]
    conv_ch = gl.shape[-1]
    hb = ts // CONV_HALO
    kern = functools.partial(_outproj_kernel, ts=ts, w_att=w_att, conv_ch=conv_ch)
    vec = lambda n: pl.BlockSpec((1, n), lambda b, i: (0, 0))
    return pl.pallas_call(
        kern,
        grid=(B, S // ts),
        in_specs=[
            pl.BlockSpec((None, ts, D), lambda b, i: (b, i, 0)),
            pl.BlockSpec((None, ts, w_att), lambda b, i: (b, i, 0)),
            pl.BlockSpec((None, ts, w_att), lambda b, i: (b, i, 0)),
            pl.BlockSpec((None, ts, conv_ch), lambda b, i: (b, i, 0)),
            pl.BlockSpec((None, CONV_HALO, conv_ch), lambda b, i: (b, jnp.maximum(i * hb - 1, 0), 0)),
            vec(w_att), vec(w_att),
            pl.BlockSpec((CONV_K, conv_ch), lambda b, i: (0, 0)),
            vec(conv_ch), vec(conv_ch), vec(conv_ch),
            pl.BlockSpec((D, D), lambda b, i: (0, 0)),
        ],
        out_specs=pl.BlockSpec((None, ts, D), lambda b, i: (b, i, 0)),
        out_shape=jax.ShapeDtypeStruct((B, S, D), F32),
        scratch_shapes=[pltpu.VMEM((ts + CONV_HALO, conv_ch), F32), pltpu.VMEM((ts, D), BF16)],
        compiler_params=_params("parallel", "parallel"),
        name="outproj",
    )(x, oa, ob, gl, gl, g_fox, g_dil, cw, cb, cng, cnb, wo)


def _ffn_kernel(x_ref, xh_ref, g_ref, wup_ref, cw_ref, cb_ref, wdn_ref, gf_ref, o_ref,
                h_scr, u_scr, acc_scr, *, ts, n_chunks, final):
    i = pl.program_id(1)

    def norm(v):
        return (v * lax.rsqrt(jnp.mean(v * v, axis=-1, keepdims=True) + EPS) * g_ref[...]).astype(BF16)

    h_scr[0:FFN_HALO, :] = norm(jnp.where(i > 0, xh_ref[...], 0.0))
    h_scr[FFN_HALO:, :] = norm(x_ref[...])
    acc_scr[...] = jnp.zeros_like(acc_scr)
    first = FFN_HALO - (FFN_CONV_K - 1)

    def chunk(c, carry):
        c2 = pl.multiple_of(c * (2 * TF), 2 * TF)
        u_scr[...] = jnp.dot(h_scr[...], wup_ref[:, pl.ds(c2, 2 * TF)], preferred_element_type=F32)
        w = cw_ref[:, pl.ds(c2, 2 * TF)]
        uc = cb_ref[:, pl.ds(c2, 2 * TF)]
        for k in range(FFN_CONV_K):
            uc = uc + w[k:k + 1, :] * u_scr[first + k:first + k + ts, :]
        gate = uc[:, :TF]
        hid = (gate * _sigmoid(gate) * uc[:, TF:]).astype(BF16)
        c1 = pl.multiple_of(c * TF, TF)
        acc_scr[...] += jnp.dot(hid, wdn_ref[pl.ds(c1, TF), :], preferred_element_type=F32)
        return carry

    lax.fori_loop(0, n_chunks, chunk, 0)
    out = x_ref[...] + acc_scr[...]
    if final:
        out = out * lax.rsqrt(jnp.mean(out * out, axis=-1, keepdims=True) + EPS) * gf_ref[...]
    o_ref[...] = out


def _ffn(x, g, wup, cw, cb, wdn, g_final, *, final):
    B, S, D = x.shape
    ts = TS_FFN
    d_ff = wdn.shape[0]
    n_chunks = d_ff // TF
    hb = ts // FFN_HALO
    kern = functools.partial(_ffn_kernel, ts=ts, n_chunks=n_chunks, final=final)
    const = lambda shape: pl.BlockSpec(shape, lambda b, i: (0,) * len(shape))
    return pl.pallas_call(
        kern,
        grid=(B, S // ts),
        in_specs=[
            pl.BlockSpec((None, ts, D), lambda b, i: (b, i, 0)),
            pl.BlockSpec((None, FFN_HALO, D), lambda b, i: (b, jnp.maximum(i * hb - 1, 0), 0)),
            const((1, D)),
            const((D, 2 * d_ff)),
            const((FFN_CONV_K, 2 * d_ff)),
            const((1, 2 * d_ff)),
            const((d_ff, D)),
            const((1, D)),
        ],
        out_specs=pl.BlockSpec((None, ts, D), lambda b, i: (b, i, 0)),
        out_shape=jax.ShapeDtypeStruct((B, S, D), F32),
        scratch_shapes=[
            pltpu.VMEM((ts + FFN_HALO, D), BF16),
            pltpu.VMEM((ts + FFN_HALO, 2 * TF), F32),
            pltpu.VMEM((ts, D), F32),
        ],
        compiler_params=_params("parallel", "parallel"),
        name="ffn_final" if final else "ffn",
    )(x, x, g, wup, cw, cb, wdn, g_final)


def _interleave_gate_value(a, d_ff):
    lead = a.shape[:-1]
    gate = a[..., :d_ff].reshape(lead + (d_ff // TF, 1, TF))
    val = a[..., d_ff:].reshape(lead + (d_ff // TF, 1, TF))
    return jnp.concatenate([gate, val], axis=-2).reshape(lead + (2 * d_ff,))


def kernel(x, ln1_g, w_in, b_forget, g_out_fox, g_out_dil, conv_w, conv_b, cnorm_g, cnorm_b,
           w_o, ln2_g, w_up, ffn_conv_w, ffn_conv_b, w_down, g_final):
    B, S, D = x.shape
    depth = w_in.shape[0]
    w_fox = g_out_fox.shape[-1]
    w_dil = g_out_dil.shape[-1]
    conv_ch = conv_w.shape[-1]
    n_heads_fox = b_forget.shape[-1]
    d_ff = w_down.shape[1]
    assert w_fox == w_dil and w_fox % LANES == 0 and n_heads_fox == w_fox // HEAD_DIM
    assert n_heads_fox <= SUBLANES and d_ff % TF == 0 and D % LANES == 0
    assert S % TM_INPROJ == 0 and S % TQ == 0 and S % TS_OUT == 0 and S % TS_FFN == 0
    n_att = w_fox // LANES
    off_fa = 3 * w_fox
    off_qb = off_fa + n_heads_fox
    off_gv = off_qb + 3 * w_dil
    assert w_in.shape[-1] == off_gv + 2 * conv_ch

    logw = _dil_log_weights(S, TQ)
    row = lambda v: v.reshape(1, -1)

    for l in range(depth):
        wl = w_in[l]
        w_f = jnp.pad(wl[:, off_fa:off_qb], ((0, 0), (0, LANES - n_heads_fox)))
        w_cat = jnp.concatenate([wl[:, :off_fa], wl[:, off_qb:], w_f], axis=1).astype(BF16)
        b_f = jnp.pad(b_forget[l], (0, LANES - n_heads_fox)).reshape(1, LANES)

        qkv, gl, f = _inproj(x, row(ln1_g[l]), w_cat, n_att=n_att, conv_ch=conv_ch)
        ccol, crow = _cumsum(f, b_f)
        oa = _fox_attention(qkv, ccol, crow, n_att=n_att)
        ob = _dil_attention(qkv, logw, n_att=n_att)
        x = _outproj(x, oa, ob, gl, row(g_out_fox[l]), row(g_out_dil[l]), conv_w[l], row(conv_b[l]),
                     row(cnorm_g[l]), row(cnorm_b[l]), w_o[l].astype(BF16))
        x = _ffn(x, row(ln2_g[l]),
                 _interleave_gate_value(w_up[l], d_ff).astype(BF16),
                 _interleave_gate_value(ffn_conv_w[l], d_ff),
                 _interleave_gate_value(row(ffn_conv_b[l]), d_ff),
                 w_down[l].astype(BF16), row(g_final), final=(l == depth - 1))
    return x
```

```python
import functools

import numpy as np
import jax
import jax.numpy as jnp
from jax import lax
from jax.experimental import pallas as pl
from jax.experimental.pallas import tpu as pltpu

F32 = jnp.float32
BF16 = jnp.bfloat16

HEAD_DIM = 64
LANES = 128
SUBLANES = 8
DILATION_PAIRS = ((128, 1), (512, 4), (2048, 16))
CONV_K = 31
FFN_CONV_K = 3
EPS = 1e-6
LOG2E = 1.4426950408889634
VMEM_LIMIT = 56 * 1024 * 1024

TM_INPROJ = 512
TQ = 256
TS_OUT = 512
CONV_HALO = 32
CONV_ROWS = 64
TS_FFN = 512
FFN_HALO = SUBLANES
TF = 256


def _params(*sem):
    return pltpu.CompilerParams(dimension_semantics=sem, vmem_limit_bytes=VMEM_LIMIT)


def _sigmoid(x):
    return 1.0 / (1.0 + jnp.exp(-x))


def _inproj_kernel(x_ref, g_ref, w_ref, qkv_ref, gl_ref, f_ref, h_scr, *, n_att, conv_ch):
    x = x_ref[...]
    ms = jnp.mean(x * x, axis=-1, keepdims=True)
    h_scr[...] = (x * lax.rsqrt(ms + EPS) * g_ref[...]).astype(BF16)
    h = h_scr[...]
    w_att = n_att * LANES
    for t in range(6):
        r = jnp.dot(h, w_ref[:, t * w_att:(t + 1) * w_att], preferred_element_type=F32)
        if t % 3 == 0:
            r = r * (HEAD_DIM ** -0.5)
        for i in range(n_att):
            qkv_ref[t * n_att + i] = r[:, i * LANES:(i + 1) * LANES].astype(BF16)
    off = 6 * w_att
    gv = jnp.dot(h, w_ref[:, off:off + conv_ch], preferred_element_type=F32)
    gg = jnp.dot(h, w_ref[:, off + conv_ch:off + 2 * conv_ch], preferred_element_type=F32)
    gl_ref[...] = (gv * _sigmoid(gg)).astype(BF16)
    off = off + 2 * conv_ch
    f_ref[...] = jnp.dot(h, w_ref[:, off:off + LANES], preferred_element_type=F32)


def _inproj(x, g, w, *, n_att, conv_ch):
    B, S, D = x.shape
    tm = TM_INPROJ
    n_cols = w.shape[1]
    kern = functools.partial(_inproj_kernel, n_att=n_att, conv_ch=conv_ch)
    return pl.pallas_call(
        kern,
        grid=(B, S // tm),
        in_specs=[
            pl.BlockSpec((None, tm, D), lambda b, i: (b, i, 0)),
            pl.BlockSpec((1, D), lambda b, i: (0, 0)),
            pl.BlockSpec((D, n_cols), lambda b, i: (0, 0)),
        ],
        out_specs=[
            pl.BlockSpec((None, 6 * n_att, tm, LANES), lambda b, i: (b, 0, i, 0)),
            pl.BlockSpec((None, tm, conv_ch), lambda b, i: (b, i, 0)),
            pl.BlockSpec((None, tm, LANES), lambda b, i: (b, i, 0)),
        ],
        out_shape=[
            jax.ShapeDtypeStruct((B, 6 * n_att, S, LANES), BF16),
            jax.ShapeDtypeStruct((B, S, conv_ch), BF16),
            jax.ShapeDtypeStruct((B, S, LANES), F32),
        ],
        scratch_shapes=[pltpu.VMEM((tm, D), BF16)],
        compiler_params=_params("parallel", "parallel"),
        name="inproj",
    )(x, g, w)


def _cumsum_kernel(f_ref, b_ref, ccol_ref, crow_ref):
    S = f_ref.shape[0]
    ri = lax.broadcasted_iota(jnp.int32, (LANES, LANES), 0)
    ci = lax.broadcasted_iota(jnp.int32, (LANES, LANES), 1)
    tri = jnp.where(ci <= ri, 1.0, 0.0).astype(BF16)
    carry = jnp.zeros((1, LANES), F32)
    for blk in range(S // LANES):
        rows = slice(blk * LANES, (blk + 1) * LANES)
        z = f_ref[rows, :] + b_ref[...]
        a = jnp.minimum(z, 0.0) - jnp.log(1.0 + jnp.exp(-jnp.abs(z)))
        hi = a.astype(BF16)
        r1 = a - hi.astype(F32)
        mid = r1.astype(BF16)
        lo = (r1 - mid.astype(F32)).astype(BF16)
        c = (jnp.dot(tri, hi, preferred_element_type=F32)
             + jnp.dot(tri, mid, preferred_element_type=F32)
             + jnp.dot(tri, lo, preferred_element_type=F32)) + carry
        ccol_ref[rows, :] = c
        crow_ref[:, rows] = jnp.transpose(c)[0:SUBLANES, :]
        carry = c[LANES - 1:LANES, :]


def _cumsum(f, bias):
    B, S, _ = f.shape
    return pl.pallas_call(
        _cumsum_kernel,
        grid=(B,),
        in_specs=[
            pl.BlockSpec((None, S, LANES), lambda b: (b, 0, 0)),
            pl.BlockSpec((1, LANES), lambda b: (0, 0)),
        ],
        out_specs=[
            pl.BlockSpec((None, S, LANES), lambda b: (b, 0, 0)),
            pl.BlockSpec((None, SUBLANES, S), lambda b: (b, 0, 0)),
        ],
        out_shape=[
            jax.ShapeDtypeStruct((B, S, LANES), F32),
            jax.ShapeDtypeStruct((B, SUBLANES, S), F32),
        ],
        compiler_params=_params("parallel"),
        name="cumsum",
    )(f, bias)


_NT = (((1,), (1,)), ((), ()))


def _attn_two_pass(q_ref, k_ref, v_ref, o_ref, scr, qi, bias_fn, *, tq, n_pairs):
    qm_scr, s_scr, mx_scr, ls_scr, acc_scr = scr
    halves = tq // LANES
    lane = lax.broadcasted_iota(jnp.int32, (tq, LANES), 1)
    for pr in range(n_pairs):
        qf = q_ref[pr].astype(F32)
        qm_scr[2 * pr] = jnp.where(lane < HEAD_DIM, qf, 0.0).astype(BF16)
        qm_scr[2 * pr + 1] = jnp.where(lane >= HEAD_DIM, qf, 0.0).astype(BF16)

    def score_block(kb, diag):
        k0 = pl.multiple_of(kb * tq, tq)
        for pr in range(n_pairs):
            kblk = k_ref[pr, pl.ds(k0, tq), :]
            for j in range(2):
                h = 2 * pr + j
                s = lax.dot_general(qm_scr[h], kblk, _NT, preferred_element_type=F32)
                cols = [bias_fn(h, c, s[:, c * LANES:(c + 1) * LANES], kb, k0, diag) * LOG2E
                        for c in range(halves)]
                s_scr[h, kb] = jnp.concatenate(cols, axis=1)
                blkmax = functools.reduce(jnp.maximum, cols)
                mx_scr[h] = blkmax if diag else jnp.maximum(mx_scr[h], blkmax)

    score_block(qi, True)

    def score_body(kb, carry):
        score_block(kb, False)
        return carry

    lax.fori_loop(0, qi, score_body, 0)

    for h in range(2 * n_pairs):
        m = jnp.max(mx_scr[h], axis=-1, keepdims=True)
        mx_scr[h] = jnp.broadcast_to(m, (tq, LANES))
        ls_scr[h] = jnp.zeros((tq, LANES), F32)
        acc_scr[h] = jnp.zeros((tq, LANES), F32)

    def value_body(kb, carry):
        k0 = pl.multiple_of(kb * tq, tq)
        for pr in range(n_pairs):
            vblk = v_ref[pr, pl.ds(k0, tq), :]
            for j in range(2):
                h = 2 * pr + j
                m = mx_scr[h]
                s = s_scr[h, kb]
                p = [jnp.exp2(s[:, c * LANES:(c + 1) * LANES] - m) for c in range(halves)]
                ls_scr[h] += functools.reduce(jnp.add, p)
                acc_scr[h] += jnp.dot(jnp.concatenate(p, axis=1).astype(BF16), vblk,
                                      preferred_element_type=F32)
        return carry

    lax.fori_loop(0, qi + 1, value_body, 0)

    for pr in range(n_pairs):
        o = []
        for j in range(2):
            h = 2 * pr + j
            o.append(acc_scr[h] / jnp.sum(ls_scr[h], axis=-1, keepdims=True))
        o_ref[:, pr * LANES:(pr + 1) * LANES] = jnp.where(lane < HEAD_DIM, o[0], o[1]).astype(o_ref.dtype)


def _causal_mask(c, s, tq):
    ri = lax.broadcasted_iota(jnp.int32, (tq, LANES), 0)
    ci = lax.broadcasted_iota(jnp.int32, (tq, LANES), 1) + c * LANES
    return jnp.where(ci <= ri, s, -jnp.inf)


def _fox_kernel(q_ref, k_ref, v_ref, ccol_ref, crow_ref, o_ref, cq_scr, *scr, tq, n_pairs):
    qi = pl.program_id(1)
    ccol_blk = ccol_ref[...]
    for h in range(2 * n_pairs):
        cq_scr[h] = jnp.broadcast_to(ccol_blk[:, h:h + 1], (tq, LANES))

    def bias_fn(h, c, s, kb, k0, diag):
        c_k = crow_ref[h:h + 1, pl.ds(k0 + c * LANES, LANES)]
        s = s + (cq_scr[h] - c_k)
        return _causal_mask(c, s, tq) if diag else s

    _attn_two_pass(q_ref, k_ref, v_ref, o_ref, scr, qi, bias_fn, tq=tq, n_pairs=n_pairs)


def _dil_kernel(q_ref, k_ref, v_ref, lw_ref, o_ref, *scr, tq, n_pairs):
    qi = pl.program_id(1)

    def bias_fn(h, c, s, kb, k0, diag):
        off = 0 if diag else qi - kb
        return s + lw_ref[off, :, c * LANES:(c + 1) * LANES]

    _attn_two_pass(q_ref, k_ref, v_ref, o_ref, scr, qi, bias_fn, tq=tq, n_pairs=n_pairs)


def _qkv_specs(S, tq, n_att, group):
    base = 3 * group
    return [
        pl.BlockSpec((None, n_att, tq, LANES), lambda b, qi: (b, base, qi, 0)),
        pl.BlockSpec((None, n_att, S, LANES), lambda b, qi: (b, base + 1, 0, 0)),
        pl.BlockSpec((None, n_att, S, LANES), lambda b, qi: (b, base + 2, 0, 0)),
    ]


def _attn_scratch(S, tq, n_att):
    n_heads = 2 * n_att
    head_tile = pltpu.VMEM((n_heads, tq, LANES), F32)
    return [
        pltpu.VMEM((n_heads, tq, LANES), BF16),
        pltpu.VMEM((n_heads, S // tq, tq, tq), F32),
        head_tile, head_tile, head_tile,
    ]


def _fox_attention(qkv, ccol, crow, *, n_att):
    B, _, S, _ = qkv.shape
    tq = TQ
    return pl.pallas_call(
        functools.partial(_fox_kernel, tq=tq, n_pairs=n_att),
        grid=(B, S // tq),
        in_specs=_qkv_specs(S, tq, n_att, 0) + [
            pl.BlockSpec((None, tq, LANES), lambda b, qi: (b, qi, 0)),
            pl.BlockSpec((None, SUBLANES, S), lambda b, qi: (b, 0, 0)),
        ],
        out_specs=pl.BlockSpec((None, tq, n_att * LANES), lambda b, qi: (b, qi, 0)),
        out_shape=jax.ShapeDtypeStruct((B, S, n_att * LANES), BF16),
        scratch_shapes=[pltpu.VMEM((2 * n_att, tq, LANES), F32)] + _attn_scratch(S, tq, n_att),
        compiler_params=_params("parallel", "arbitrary"),
        name="fox_attn",
    )(qkv, qkv, qkv, ccol, crow)


def _dil_log_weights(S, tq):
    for window, d in DILATION_PAIRS:
        assert S % ((window // d) * d) == 0, "dilated branches assume no sequence padding"
    nq = S // tq
    r = np.arange(tq)[:, None]
    c = np.arange(tq)[None, :]
    tabs = []
    for off in range(nq):
        delta = off * tq + r - c
        w = np.zeros((tq, tq), np.int64)
        for window, d in DILATION_PAIRS:
            w += (delta >= 0) & (delta % d == 0) & (delta // d <= window // d)
        with np.errstate(divide="ignore"):
            tabs.append(np.log(w.astype(np.float64)))
    return jnp.asarray(np.stack(tabs).astype(np.float32))


def _dil_attention(qkv, logw, *, n_att):
    B, _, S, _ = qkv.shape
    tq = TQ
    nq = S // tq
    return pl.pallas_call(
        functools.partial(_dil_kernel, tq=tq, n_pairs=n_att),
        grid=(B, nq),
        in_specs=_qkv_specs(S, tq, n_att, 1) + [
            pl.BlockSpec((nq, tq, tq), lambda b, qi: (0, 0, 0)),
        ],
        out_specs=pl.BlockSpec((None, tq, n_att * LANES), lambda b, qi: (b, qi, 0)),
        out_shape=jax.ShapeDtypeStruct((B, S, n_att * LANES), BF16),
        scratch_shapes=_attn_scratch(S, tq, n_att),
        compiler_params=_params("parallel", "arbitrary"),
        name="dil_attn",
    )(qkv, qkv, qkv, logw)


def _shift_matrices():
    k = CONV_ROWS + CONV_HALO
    s = np.zeros((SUBLANES, k, k), np.float32)
    for r in range(SUBLANES):
        s[r, np.arange(k - r), np.arange(k - r) + r] = 1.0
    return jnp.asarray(s, dtype=BF16)


def _outproj_kernel(x_ref, oa_ref, ob_ref, gl_ref, glh_ref, sh_ref, gfox_ref, gdil_ref, cw_ref, cb_ref,
                    cng_ref, cnb_ref, wo_ref, o_ref, ext_scr, y_scr, *, ts, w_att, conv_ch):
    i = pl.program_id(1)
    ext_scr[0:CONV_HALO, :] = jnp.where(i > 0, glh_ref[...], jnp.zeros_like(glh_ref))
    ext_scr[CONV_HALO:, :] = gl_ref[...]
    first = CONV_HALO - (CONV_K - 1)
    for c in range(ts // CONV_ROWS):
        r0 = c * CONV_ROWS
        win = ext_scr[r0:r0 + CONV_ROWS + CONV_HALO, :]
        acc = jnp.broadcast_to(cb_ref[...], (CONV_ROWS, conv_ch))
        for r in range(SUBLANES):
            shifted = jnp.dot(sh_ref[r], win, preferred_element_type=F32)
            for k in range(CONV_K):
                e = first + k
                if e % SUBLANES == r:
                    a0 = e - r
                    acc = acc + cw_ref[k:k + 1, :] * shifted[a0:a0 + CONV_ROWS, :]
        mu = jnp.mean(acc, axis=-1, keepdims=True)
        xc = acc - mu
        yn = xc * lax.rsqrt(jnp.mean(xc * xc, axis=-1, keepdims=True) + EPS)
        yn = yn * cng_ref[...] + cnb_ref[...]
        y_scr[r0:r0 + CONV_ROWS, 2 * w_att:] = (yn * _sigmoid(yn)).astype(BF16)
    for o_in, g_in, lo in ((oa_ref, gfox_ref, 0), (ob_ref, gdil_ref, w_att)):
        a = o_in[...].astype(F32)
        ms = jnp.mean(a * a, axis=-1, keepdims=True)
        y_scr[:, lo:lo + w_att] = (a * lax.rsqrt(ms + EPS) * g_in[...]).astype(BF16)
    o_ref[...] = x_ref[...] + jnp.dot(y_scr[...], wo_ref[...], preferred_element_type=F32)


def _outproj(x, oa, ob, gl, g_fox, g_dil, cw, cb, cng, cnb, wo):
    B, S, D = x.shape
    ts = TS_OUT
    w_att = oa.shape[-1]
    conv_ch = gl.shape[-1]
    hb = ts // CONV_HALO
    shifts = _shift_matrices()
    kern = functools.partial(_outproj_kernel, ts=ts, w_att=w_att, conv_ch=conv_ch)
    vec = lambda n: pl.BlockSpec((1, n), lambda b, i: (0, 0))
    return pl.pallas_call(
        kern,
        grid=(B, S // ts),
        in_specs=[
            pl.BlockSpec((None, ts, D), lambda b, i: (b, i, 0)),
            pl.BlockSpec((None, ts, w_att), lambda b, i: (b, i, 0)),
            pl.BlockSpec((None, ts, w_att), lambda b, i: (b, i, 0)),
            pl.BlockSpec((None, ts, conv_ch), lambda b, i: (b, i, 0)),
            pl.BlockSpec((None, CONV_HALO, conv_ch), lambda b, i: (b, jnp.maximum(i * hb - 1, 0), 0)),
            pl.BlockSpec(shifts.shape, lambda b, i: (0, 0, 0)),
            vec(w_att), vec(w_att),
            pl.BlockSpec((CONV_K, conv_ch), lambda b, i: (0, 0)),
            vec(conv_ch), vec(conv_ch), vec(conv_ch),
            pl.BlockSpec((D, D), lambda b, i: (0, 0)),
        ],
        out_specs=pl.BlockSpec((None, ts, D), lambda b, i: (b, i, 0)),
        out_shape=jax.ShapeDtypeStruct((B, S, D), F32),
        scratch_shapes=[pltpu.VMEM((ts + CONV_HALO, conv_ch), BF16), pltpu.VMEM((ts, D), BF16)],
        compiler_params=_params("parallel", "parallel"),
        name="outproj",
    )(x, oa, ob, gl, gl, shifts, g_fox, g_dil, cw, cb, cng, cnb, wo)


def _ffn_kernel(x_ref, xh_ref, g_ref, wup_ref, cw_ref, cb_ref, wdn_ref, gf_ref, o_ref,
                h_scr, hid_scr, *, ts, d_ff, final):
    i = pl.program_id(1)
    n_chunks = d_ff // TF

    def norm(v):
        return (v * lax.rsqrt(jnp.mean(v * v, axis=-1, keepdims=True) + EPS) * g_ref[...]).astype(BF16)

    h_scr[0:FFN_HALO, :] = norm(jnp.where(i > 0, xh_ref[...], 0.0))
    h_scr[FFN_HALO:, :] = norm(x_ref[...])
    first = FFN_HALO - (FFN_CONV_K - 1)

    def cols(c, part):
        return pl.ds(pl.multiple_of(part * d_ff + c * TF, TF), TF)

    def chunk(c):
        halves = []
        for part in range(2):
            u = jnp.dot(h_scr[...], wup_ref[:, cols(c, part)], preferred_element_type=F32)
            w = cw_ref[:, cols(c, part)]
            uc = cb_ref[:, cols(c, part)]
            for k in range(FFN_CONV_K):
                uc = uc + w[k:k + 1, :] * u[first + k:first + k + ts, :]
            halves.append(uc)
        gate, val = halves
        hid_scr[c] = (gate * _sigmoid(gate) * val).astype(BF16)

    def pair(j, carry):
        chunk(2 * j)
        chunk(2 * j + 1)
        return carry

    lax.fori_loop(0, n_chunks // 2, pair, 0)
    if n_chunks % 2:
        chunk(n_chunks - 1)

    acc = jnp.dot(hid_scr[0], wdn_ref[0:TF, :], preferred_element_type=F32)
    for c in range(1, n_chunks):
        acc = acc + jnp.dot(hid_scr[c], wdn_ref[c * TF:(c + 1) * TF, :], preferred_element_type=F32)
    out = x_ref[...] + acc
    if final:
        out = out * lax.rsqrt(jnp.mean(out * out, axis=-1, keepdims=True) + EPS) * gf_ref[...]
    o_ref[...] = out


def _ffn(x, g, wup, cw, cb, wdn, g_final, *, final):
    B, S, D = x.shape
    ts = TS_FFN
    d_ff = wdn.shape[0]
    hb = ts // FFN_HALO
    kern = functools.partial(_ffn_kernel, ts=ts, d_ff=d_ff, final=final)
    const = lambda shape: pl.BlockSpec(shape, lambda b, i: (0,) * len(shape))
    return pl.pallas_call(
        kern,
        grid=(B, S // ts),
        in_specs=[
            pl.BlockSpec((None, ts, D), lambda b, i: (b, i, 0)),
            pl.BlockSpec((None, FFN_HALO, D), lambda b, i: (b, jnp.maximum(i * hb - 1, 0), 0)),
            const((1, D)),
            const((D, 2 * d_ff)),
            const((FFN_CONV_K, 2 * d_ff)),
            const((1, 2 * d_ff)),
            const((d_ff, D)),
            const((1, D)),
        ],
        out_specs=pl.BlockSpec((None, ts, D), lambda b, i: (b, i, 0)),
        out_shape=jax.ShapeDtypeStruct((B, S, D), F32),
        scratch_shapes=[
            pltpu.VMEM((ts + FFN_HALO, D), BF16),
            pltpu.VMEM((d_ff // TF, ts, TF), BF16),
        ],
        compiler_params=_params("parallel", "parallel"),
        name="ffn_final" if final else "ffn",
    )(x, x, g, wup, cw, cb, wdn, g_final)


def kernel(x, ln1_g, w_in, b_forget, g_out_fox, g_out_dil, conv_w, conv_b, cnorm_g, cnorm_b,
           w_o, ln2_g, w_up, ffn_conv_w, ffn_conv_b, w_down, g_final):
    B, S, D = x.shape
    depth = w_in.shape[0]
    w_fox = g_out_fox.shape[-1]
    w_dil = g_out_dil.shape[-1]
    conv_ch = conv_w.shape[-1]
    n_heads_fox = b_forget.shape[-1]
    d_ff = w_down.shape[1]
    assert w_fox == w_dil and w_fox % LANES == 0 and n_heads_fox == w_fox // HEAD_DIM
    assert n_heads_fox <= SUBLANES and d_ff % TF == 0 and D % LANES == 0
    assert S % TM_INPROJ == 0 and S % TQ == 0 and S % TS_OUT == 0 and S % TS_FFN == 0
    assert CONV_K - 1 <= CONV_HALO and TS_OUT % CONV_ROWS == 0 and CONV_ROWS % CONV_HALO == 0
    n_att = w_fox // LANES
    off_fa = 3 * w_fox
    off_qb = off_fa + n_heads_fox
    off_gv = off_qb + 3 * w_dil
    assert w_in.shape[-1] == off_gv + 2 * conv_ch

    logw = _dil_log_weights(S, TQ)
    row = lambda v: v.reshape(1, -1)

    for l in range(depth):
        wl = w_in[l]
        w_f = jnp.pad(wl[:, off_fa:off_qb], ((0, 0), (0, LANES - n_heads_fox)))
        w_cat = jnp.concatenate([wl[:, :off_fa], wl[:, off_qb:], w_f], axis=1).astype(BF16)
        b_f = jnp.pad(b_forget[l], (0, LANES - n_heads_fox)).reshape(1, LANES)

        qkv, gl, f = _inproj(x, row(ln1_g[l]), w_cat, n_att=n_att, conv_ch=conv_ch)
        ccol, crow = _cumsum(f, b_f)
        oa = _fox_attention(qkv, ccol, crow, n_att=n_att)
        ob = _dil_attention(qkv, logw, n_att=n_att)
        x = _outproj(x, oa, ob, gl, row(g_out_fox[l]), row(g_out_dil[l]), conv_w[l], row(conv_b[l]),
                     row(cnorm_g[l]), row(cnorm_b[l]), w_o[l].astype(BF16))
        x = _ffn(x, row(ln2_g[l]), w_up[l].astype(BF16), ffn_conv_w[l], row(ffn_conv_b[l]),
                 w_down[l].astype(BF16), row(g_final), final=(l == depth - 1))
    return x
```

```python
import functools

import numpy as np
import jax
import jax.numpy as jnp
from jax import lax
from jax.experimental import pallas as pl
from jax.experimental.pallas import tpu as pltpu

F32 = jnp.float32
BF16 = jnp.bfloat16

HEAD_DIM = 64
LANES = 128
SUBLANES = 8
DILATION_PAIRS = ((128, 1), (512, 4), (2048, 16))
CONV_K = 31
FFN_CONV_K = 3
EPS = 1e-6
LOG2E = 1.4426950408889634
VMEM_LIMIT = 56 * 1024 * 1024

TM_INPROJ = 512
TQ = 256
TS_OUT = 512
CONV_HALO = 32
CONV_ROWS = 64
TS_FFN = 512
FFN_HALO = 16
FFN_ROWS = 256
TF = 256


def _params(*sem):
    return pltpu.CompilerParams(dimension_semantics=sem, vmem_limit_bytes=VMEM_LIMIT)


def _sigmoid(x):
    return 1.0 / (1.0 + jnp.exp(-x))


def _inproj_kernel(x_ref, g_ref, wa_ref, wb_ref, wf_ref, qkv_ref, gl_ref, f_ref, h_scr, *, n_att, conv_ch):
    x = x_ref[...]
    ms = jnp.mean(x * x, axis=-1, keepdims=True)
    h_scr[...] = (x * lax.rsqrt(ms + EPS) * g_ref[...]).astype(BF16)
    h = h_scr[...]
    w_att = n_att * LANES
    for t in range(6):
        w_ref, tt = (wa_ref, t) if t < 3 else (wb_ref, t - 3)
        r = jnp.dot(h, w_ref[:, tt * w_att:(tt + 1) * w_att], preferred_element_type=F32)
        if t % 3 == 0:
            r = r * (HEAD_DIM ** -0.5 * LOG2E)
        for i in range(n_att):
            qkv_ref[t * n_att + i] = r[:, i * LANES:(i + 1) * LANES].astype(BF16)
    off = 3 * w_att
    gv = jnp.dot(h, wb_ref[:, off:off + conv_ch], preferred_element_type=F32)
    gg = jnp.dot(h, wb_ref[:, off + conv_ch:off + 2 * conv_ch], preferred_element_type=F32)
    gl_ref[...] = (gv * _sigmoid(gg)).astype(BF16)
    f_ref[...] = jnp.dot(h, wf_ref[...], preferred_element_type=F32)


def _inproj(x, g, wa, wb, wf, *, n_att, conv_ch):
    B, S, D = x.shape
    tm = TM_INPROJ
    kern = functools.partial(_inproj_kernel, n_att=n_att, conv_ch=conv_ch)
    return pl.pallas_call(
        kern,
        grid=(B, S // tm),
        in_specs=[
            pl.BlockSpec((None, tm, D), lambda b, i: (b, i, 0)),
            pl.BlockSpec((1, D), lambda b, i: (0, 0)),
            pl.BlockSpec(wa.shape, lambda b, i: (0, 0)),
            pl.BlockSpec(wb.shape, lambda b, i: (0, 0)),
            pl.BlockSpec(wf.shape, lambda b, i: (0, 0)),
        ],
        out_specs=[
            pl.BlockSpec((None, 6 * n_att, tm, LANES), lambda b, i: (b, 0, i, 0)),
            pl.BlockSpec((None, tm, conv_ch), lambda b, i: (b, i, 0)),
            pl.BlockSpec((None, tm, LANES), lambda b, i: (b, i, 0)),
        ],
        out_shape=[
            jax.ShapeDtypeStruct((B, 6 * n_att, S, LANES), BF16),
            jax.ShapeDtypeStruct((B, S, conv_ch), BF16),
            jax.ShapeDtypeStruct((B, S, LANES), F32),
        ],
        scratch_shapes=[pltpu.VMEM((tm, D), BF16)],
        compiler_params=_params("parallel", "parallel"),
        name="inproj",
    )(x, g, wa, wb, wf)


def _cumsum_kernel(f_ref, b_ref, ccol_ref, crow_ref):
    S = f_ref.shape[0]
    ri = lax.broadcasted_iota(jnp.int32, (LANES, LANES), 0)
    ci = lax.broadcasted_iota(jnp.int32, (LANES, LANES), 1)
    tri = jnp.where(ci <= ri, 1.0, 0.0).astype(BF16)
    carry = jnp.zeros((1, LANES), F32)
    for blk in range(S // LANES):
        rows = slice(blk * LANES, (blk + 1) * LANES)
        z = f_ref[rows, :] + b_ref[...]
        a = jnp.minimum(z, 0.0) - jnp.log(1.0 + jnp.exp(-jnp.abs(z)))
        hi = a.astype(BF16)
        r1 = a - hi.astype(F32)
        mid = r1.astype(BF16)
        lo = (r1 - mid.astype(F32)).astype(BF16)
        c = (jnp.dot(tri, hi, preferred_element_type=F32)
             + jnp.dot(tri, mid, preferred_element_type=F32)
             + jnp.dot(tri, lo, preferred_element_type=F32)) + carry
        ccol_ref[rows, :] = c * LOG2E
        crow_ref[:, rows] = jnp.transpose(c * LOG2E)[0:SUBLANES, :]
        carry = c[LANES - 1:LANES, :]


def _cumsum(f, bias):
    B, S, _ = f.shape
    return pl.pallas_call(
        _cumsum_kernel,
        grid=(B,),
        in_specs=[
            pl.BlockSpec((None, S, LANES), lambda b: (b, 0, 0)),
            pl.BlockSpec((1, LANES), lambda b: (0, 0)),
        ],
        out_specs=[
            pl.BlockSpec((None, S, LANES), lambda b: (b, 0, 0)),
            pl.BlockSpec((None, SUBLANES, S), lambda b: (b, 0, 0)),
        ],
        out_shape=[
            jax.ShapeDtypeStruct((B, S, LANES), F32),
            jax.ShapeDtypeStruct((B, SUBLANES, S), F32),
        ],
        compiler_params=_params("parallel"),
        name="cumsum",
    )(f, bias)


_NT = (((1,), (1,)), ((), ()))


def _run_blocks(n, block_fn):
    def pair(j, carry):
        block_fn(2 * j)
        block_fn(2 * j + 1)
        return carry

    lax.fori_loop(0, lax.shift_right_logical(n, 1), pair, 0)

    @pl.when(jnp.bitwise_and(n, 1) == 1)
    def _():
        block_fn(n - 1)


def _attn_two_pass(k_block, v_ref, o_ref, scr, qi, bias_fn, *, tq, n_pairs):
    qm_scr, s_scr, mx_scr, ls_scr, acc_scr = scr
    halves = tq // LANES
    lane = lax.broadcasted_iota(jnp.int32, (tq, LANES), 1)

    def score_block(kb, diag):
        k0 = pl.multiple_of(kb * tq, tq)
        for h in range(2 * n_pairs):
            s = lax.dot_general(qm_scr[h], k_block(h, k0), _NT, preferred_element_type=F32)
            cols = [bias_fn(h, c, s[:, c * LANES:(c + 1) * LANES], kb, diag) for c in range(halves)]
            s_scr[h, kb] = jnp.concatenate(cols, axis=1)
            blkmax = functools.reduce(jnp.maximum, cols)
            mx_scr[h] = blkmax if diag else jnp.maximum(mx_scr[h], blkmax)

    score_block(qi, True)
    _run_blocks(qi, lambda kb: score_block(kb, False))

    for h in range(2 * n_pairs):
        m = jnp.max(mx_scr[h], axis=-1, keepdims=True)
        mx_scr[h] = jnp.broadcast_to(m, (tq, LANES))
        ls_scr[h] = jnp.zeros((tq, LANES), F32)
        acc_scr[h] = jnp.zeros((tq, LANES), F32)

    def value_block(kb):
        k0 = pl.multiple_of(kb * tq, tq)
        for h in range(2 * n_pairs):
            m = mx_scr[h]
            s = s_scr[h, kb]
            p = [jnp.exp2(s[:, c * LANES:(c + 1) * LANES] - m) for c in range(halves)]
            ls_scr[h] += functools.reduce(jnp.add, p)
            acc_scr[h] += jnp.dot(jnp.concatenate(p, axis=1).astype(BF16), v_ref[h // 2, pl.ds(k0, tq), :],
                                  preferred_element_type=F32)

    _run_blocks(qi + 1, value_block)

    for pr in range(n_pairs):
        o = []
        for j in range(2):
            h = 2 * pr + j
            o.append(acc_scr[h] / jnp.sum(ls_scr[h], axis=-1, keepdims=True))
        o_ref[:, pr * LANES:(pr + 1) * LANES] = jnp.where(lane < HEAD_DIM, o[0], o[1]).astype(o_ref.dtype)


def _causal_mask(c, s, tq):
    ri = lax.broadcasted_iota(jnp.int32, (tq, LANES), 0)
    ci = lax.broadcasted_iota(jnp.int32, (tq, LANES), 1) + c * LANES
    return jnp.where(ci <= ri, s, -jnp.inf)


def _mask_head_queries(q_ref, qm_scr, tq, n_pairs):
    lane = lax.broadcasted_iota(jnp.int32, (tq, LANES), 1)
    for pr in range(n_pairs):
        qf = q_ref[pr].astype(F32)
        qm_scr[2 * pr] = jnp.where(lane < HEAD_DIM, qf, 0.0).astype(BF16)
        qm_scr[2 * pr + 1] = jnp.where(lane >= HEAD_DIM, qf, 0.0).astype(BF16)


def _fox_kernel(q_ref, k_ref, v_ref, ccol_ref, crow_ref, o_ref, cq_scr, *scr, tq, n_pairs):
    qi = pl.program_id(1)
    _mask_head_queries(q_ref, scr[0], tq, n_pairs)
    ccol_blk = ccol_ref[...]
    for h in range(2 * n_pairs):
        cq_scr[h] = jnp.broadcast_to(ccol_blk[:, h:h + 1], (tq, LANES))

    def bias_fn(h, c, s, kb, diag):
        k0 = pl.multiple_of(kb * tq, tq)
        c_k = crow_ref[h:h + 1, pl.ds(k0 + c * LANES, LANES)]
        s = (s + cq_scr[h]) - c_k
        return _causal_mask(c, s, tq) if diag else s

    k_block = lambda h, k0: k_ref[h // 2, pl.ds(k0, tq), :]
    _attn_two_pass(k_block, v_ref, o_ref, scr, qi, bias_fn, tq=tq, n_pairs=n_pairs)


def _dil_kernel(q_ref, k_ref, v_ref, lw_ref, o_ref, *scr, tq, n_pairs):
    qi = pl.program_id(1)
    _mask_head_queries(q_ref, scr[0], tq, n_pairs)

    def bias_fn(h, c, s, kb, diag):
        off = 0 if diag else qi - kb
        return s + lw_ref[off, :, c * LANES:(c + 1) * LANES]

    k_block = lambda h, k0: k_ref[h // 2, pl.ds(k0, tq), :]
    _attn_two_pass(k_block, v_ref, o_ref, scr, qi, bias_fn, tq=tq, n_pairs=n_pairs)


def _qkv_specs(S, tq, n_att, group):
    base = 3 * group
    return [
        pl.BlockSpec((None, n_att, tq, LANES), lambda b, qi: (b, base, qi, 0)),
        pl.BlockSpec((None, n_att, S, LANES), lambda b, qi: (b, base + 1, 0, 0)),
        pl.BlockSpec((None, n_att, S, LANES), lambda b, qi: (b, base + 2, 0, 0)),
    ]


def _attn_scratch(S, tq, n_att):
    n_heads = 2 * n_att
    head_tile = pltpu.VMEM((n_heads, tq, LANES), F32)
    return [
        pltpu.VMEM((n_heads, tq, LANES), BF16),
        pltpu.VMEM((n_heads, S // tq, tq, tq), F32),
        head_tile, head_tile, head_tile,
    ]


def _fox_attention(qkv, ccol, crow, *, n_att):
    B, _, S, _ = qkv.shape
    tq = TQ
    return pl.pallas_call(
        functools.partial(_fox_kernel, tq=tq, n_pairs=n_att),
        grid=(B, S // tq),
        in_specs=_qkv_specs(S, tq, n_att, 0) + [
            pl.BlockSpec((None, tq, LANES), lambda b, qi: (b, qi, 0)),
            pl.BlockSpec((None, SUBLANES, S), lambda b, qi: (b, 0, 0)),
        ],
        out_specs=pl.BlockSpec((None, tq, n_att * LANES), lambda b, qi: (b, qi, 0)),
        out_shape=jax.ShapeDtypeStruct((B, S, n_att * LANES), BF16),
        scratch_shapes=[pltpu.VMEM((2 * n_att, tq, LANES), F32)] + _attn_scratch(S, tq, n_att),
        compiler_params=_params("parallel", "arbitrary"),
        name="fox_attn",
    )(qkv, qkv, qkv, ccol, crow)


def _dil_log_weights(S, tq):
    for window, d in DILATION_PAIRS:
        assert S % ((window // d) * d) == 0, "dilated branches assume no sequence padding"
    nq = S // tq
    r = np.arange(tq)[:, None]
    c = np.arange(tq)[None, :]
    tabs = []
    for off in range(nq):
        delta = off * tq + r - c
        w = np.zeros((tq, tq), np.int64)
        for window, d in DILATION_PAIRS:
            w += (delta >= 0) & (delta % d == 0) & (delta // d <= window // d)
        with np.errstate(divide="ignore"):
            tabs.append(np.log2(w.astype(np.float64)))
    return jnp.asarray(np.stack(tabs).astype(np.float32))


def _dil_attention(qkv, logw, *, n_att):
    B, _, S, _ = qkv.shape
    tq = TQ
    nq = S // tq
    return pl.pallas_call(
        functools.partial(_dil_kernel, tq=tq, n_pairs=n_att),
        grid=(B, nq),
        in_specs=_qkv_specs(S, tq, n_att, 1) + [
            pl.BlockSpec((nq, tq, tq), lambda b, qi: (0, 0, 0)),
        ],
        out_specs=pl.BlockSpec((None, tq, n_att * LANES), lambda b, qi: (b, qi, 0)),
        out_shape=jax.ShapeDtypeStruct((B, S, n_att * LANES), BF16),
        scratch_shapes=_attn_scratch(S, tq, n_att),
        compiler_params=_params("parallel", "arbitrary"),
        name="dil_attn",
    )(qkv, qkv, qkv, logw)


def _shift_matrices():
    k = CONV_ROWS + CONV_HALO
    s = np.zeros((SUBLANES, k, k), np.float32)
    for r in range(SUBLANES):
        s[r, np.arange(k - r), np.arange(k - r) + r] = 1.0
    return jnp.asarray(s, dtype=BF16)


def _outproj_kernel(x_ref, oa_ref, ob_ref, gl_ref, glh_ref, sh_ref, gfox_ref, gdil_ref, cw_ref, cb_ref,
                    cng_ref, cnb_ref, wo_ref, o_ref, ext_scr, y_scr, *, ts, w_att, conv_ch):
    i = pl.program_id(1)
    ext_scr[0:CONV_HALO, :] = jnp.where(i > 0, glh_ref[...], jnp.zeros_like(glh_ref))
    ext_scr[CONV_HALO:, :] = gl_ref[...]
    first = CONV_HALO - (CONV_K - 1)
    for c in range(ts // CONV_ROWS):
        r0 = c * CONV_ROWS
        win = ext_scr[r0:r0 + CONV_ROWS + CONV_HALO, :]
        acc = jnp.broadcast_to(cb_ref[...], (CONV_ROWS, conv_ch))
        for r in range(SUBLANES):
            shifted = jnp.dot(sh_ref[r], win, preferred_element_type=F32)
            for k in range(CONV_K):
                e = first + k
                if e % SUBLANES == r:
                    a0 = e - r
                    acc = acc + cw_ref[k:k + 1, :] * shifted[a0:a0 + CONV_ROWS, :]
        mu = jnp.mean(acc, axis=-1, keepdims=True)
        xc = acc - mu
        yn = xc * lax.rsqrt(jnp.mean(xc * xc, axis=-1, keepdims=True) + EPS)
        yn = yn * cng_ref[...] + cnb_ref[...]
        y_scr[r0:r0 + CONV_ROWS, 2 * w_att:] = (yn * _sigmoid(yn)).astype(BF16)
    for o_in, g_in, lo in ((oa_ref, gfox_ref, 0), (ob_ref, gdil_ref, w_att)):
        a = o_in[...].astype(F32)
        ms = jnp.mean(a * a, axis=-1, keepdims=True)
        y_scr[:, lo:lo + w_att] = (a * lax.rsqrt(ms + EPS) * g_in[...]).astype(BF16)
    o_ref[...] = x_ref[...] + jnp.dot(y_scr[...], wo_ref[...], preferred_element_type=F32)


def _outproj(x, oa, ob, gl, g_fox, g_dil, cw, cb, cng, cnb, wo):
    B, S, D = x.shape
    ts = TS_OUT
    w_att = oa.shape[-1]
    conv_ch = gl.shape[-1]
    hb = ts // CONV_HALO
    shifts = _shift_matrices()
    kern = functools.partial(_outproj_kernel, ts=ts, w_att=w_att, conv_ch=conv_ch)
    vec = lambda n: pl.BlockSpec((1, n), lambda b, i: (0, 0))
    return pl.pallas_call(
        kern,
        grid=(B, S // ts),
        in_specs=[
            pl.BlockSpec((None, ts, D), lambda b, i: (b, i, 0)),
            pl.BlockSpec((None, ts, w_att), lambda b, i: (b, i, 0)),
            pl.BlockSpec((None, ts, w_att), lambda b, i: (b, i, 0)),
            pl.BlockSpec((None, ts, conv_ch), lambda b, i: (b, i, 0)),
            pl.BlockSpec((None, CONV_HALO, conv_ch), lambda b, i: (b, jnp.maximum(i * hb - 1, 0), 0)),
            pl.BlockSpec(shifts.shape, lambda b, i: (0, 0, 0)),
            vec(w_att), vec(w_att),
            pl.BlockSpec((CONV_K, conv_ch), lambda b, i: (0, 0)),
            vec(conv_ch), vec(conv_ch), vec(conv_ch),
            pl.BlockSpec((D, D), lambda b, i: (0, 0)),
        ],
        out_specs=pl.BlockSpec((None, ts, D), lambda b, i: (b, i, 0)),
        out_shape=jax.ShapeDtypeStruct((B, S, D), F32),
        scratch_shapes=[pltpu.VMEM((ts + CONV_HALO, conv_ch), BF16), pltpu.VMEM((ts, D), BF16)],
        compiler_params=_params("parallel", "parallel"),
        name="outproj",
    )(x, oa, ob, gl, gl, shifts, g_fox, g_dil, cw, cb, cng, cnb, wo)


def _ffn_kernel(x_ref, xh_ref, g_ref, wup_ref, cw_ref, cb_ref, wdn_ref, gf_ref, o_ref,
                h_scr, hid_scr, *, ts, d_ff, final):
    i = pl.program_id(1)
    n_chunks = d_ff // TF

    def norm(v):
        return (v * lax.rsqrt(jnp.mean(v * v, axis=-1, keepdims=True) + EPS) * g_ref[...]).astype(BF16)

    h_scr[0:FFN_HALO, :] = norm(jnp.where(i > 0, xh_ref[...], 0.0))
    h_scr[FFN_HALO:, :] = norm(x_ref[...])
    first = FFN_HALO - (FFN_CONV_K - 1)

    def cols(c, part):
        return pl.ds(pl.multiple_of(part * d_ff + c * TF, TF), TF)

    def chunk(c):
        for r0 in range(0, ts, FFN_ROWS):
            hb = h_scr[r0:r0 + FFN_ROWS + FFN_HALO, :]
            halves = []
            for part in range(2):
                u = jnp.dot(hb, wup_ref[:, cols(c, part)], preferred_element_type=F32)
                w = cw_ref[:, cols(c, part)]
                uc = cb_ref[:, cols(c, part)]
                for k in range(FFN_CONV_K):
                    uc = uc + w[k:k + 1, :] * u[first + k:first + k + FFN_ROWS, :]
                halves.append(uc)
            gate, val = halves
            hid_scr[c, r0:r0 + FFN_ROWS, :] = (gate * _sigmoid(gate) * val).astype(BF16)

    def pair(j, carry):
        chunk(2 * j)
        chunk(2 * j + 1)
        return carry

    lax.fori_loop(0, n_chunks // 2, pair, 0)
    if n_chunks % 2:
        chunk(n_chunks - 1)

    acc = jnp.dot(hid_scr[0], wdn_ref[0:TF, :], preferred_element_type=F32)
    for c in range(1, n_chunks):
        acc = acc + jnp.dot(hid_scr[c], wdn_ref[c * TF:(c + 1) * TF, :], preferred_element_type=F32)
    out = x_ref[...] + acc
    if final:
        out = out * lax.rsqrt(jnp.mean(out * out, axis=-1, keepdims=True) + EPS) * gf_ref[...]
    o_ref[...] = out


def _ffn(x, g, wup, cw, cb, wdn, g_final, *, final):
    B, S, D = x.shape
    ts = TS_FFN
    d_ff = wdn.shape[0]
    hb = ts // FFN_HALO
    kern = functools.partial(_ffn_kernel, ts=ts, d_ff=d_ff, final=final)
    const = lambda shape: pl.BlockSpec(shape, lambda b, i: (0,) * len(shape))
    return pl.pallas_call(
        kern,
        grid=(B, S // ts),
        in_specs=[
            pl.BlockSpec((None, ts, D), lambda b, i: (b, i, 0)),
            pl.BlockSpec((None, FFN_HALO, D), lambda b, i: (b, jnp.maximum(i * hb - 1, 0), 0)),
            const((1, D)),
            const((D, 2 * d_ff)),
            const((FFN_CONV_K, 2 * d_ff)),
            const((1, 2 * d_ff)),
            const((d_ff, D)),
            const((1, D)),
        ],
        out_specs=pl.BlockSpec((None, ts, D), lambda b, i: (b, i, 0)),
        out_shape=jax.ShapeDtypeStruct((B, S, D), F32),
        scratch_shapes=[
            pltpu.VMEM((ts + FFN_HALO, D), BF16),
            pltpu.VMEM((d_ff // TF, ts, TF), BF16),
        ],
        compiler_params=_params("parallel", "parallel"),
        name="ffn_final" if final else "ffn",
    )(x, x, g, wup, cw, cb, wdn, g_final)


def kernel(x, ln1_g, w_in, b_forget, g_out_fox, g_out_dil, conv_w, conv_b, cnorm_g, cnorm_b,
           w_o, ln2_g, w_up, ffn_conv_w, ffn_conv_b, w_down, g_final):
    B, S, D = x.shape
    depth = w_in.shape[0]
    w_fox = g_out_fox.shape[-1]
    w_dil = g_out_dil.shape[-1]
    conv_ch = conv_w.shape[-1]
    n_heads_fox = b_forget.shape[-1]
    d_ff = w_down.shape[1]
    assert w_fox == w_dil and w_fox % LANES == 0 and n_heads_fox == w_fox // HEAD_DIM
    assert n_heads_fox <= SUBLANES and d_ff % TF == 0 and D % LANES == 0
    assert S % TM_INPROJ == 0 and S % TQ == 0 and S % TS_OUT == 0 and S % TS_FFN == 0
    assert CONV_K - 1 <= CONV_HALO and TS_OUT % CONV_ROWS == 0 and CONV_ROWS % CONV_HALO == 0
    n_att = w_fox // LANES
    off_fa = 3 * w_fox
    off_qb = off_fa + n_heads_fox
    off_gv = off_qb + 3 * w_dil
    assert w_in.shape[-1] == off_gv + 2 * conv_ch

    logw = _dil_log_weights(S, TQ)
    row = lambda v: v.reshape(1, -1)

    for l in range(depth):
        w_a = w_in[l, :, :off_fa].astype(BF16)
        w_b = w_in[l, :, off_qb:].astype(BF16)
        w_f = jnp.pad(w_in[l, :, off_fa:off_qb], ((0, 0), (0, LANES - n_heads_fox))).astype(BF16)
        b_f = jnp.pad(b_forget[l], (0, LANES - n_heads_fox)).reshape(1, LANES)

        qkv, gl, f = _inproj(x, row(ln1_g[l]), w_a, w_b, w_f, n_att=n_att, conv_ch=conv_ch)
        ccol, crow = _cumsum(f, b_f)
        oa = _fox_attention(qkv, ccol, crow, n_att=n_att)
        ob = _dil_attention(qkv, logw, n_att=n_att)
        x = _outproj(x, oa, ob, gl, row(g_out_fox[l]), row(g_out_dil[l]), conv_w[l], row(conv_b[l]),
                     row(cnorm_g[l]), row(cnorm_b[l]), w_o[l].astype(BF16))
        x = _ffn(x, row(ln2_g[l]), w_up[l].astype(BF16), ffn_conv_w[l], row(ffn_conv_b[l]),
                 w_down[l].astype(BF16), row(g_final), final=(l == depth - 1))
    return x
```

```python
import functools

import numpy as np
import jax
import jax.numpy as jnp
from jax import lax
from jax.experimental import pallas as pl
from jax.experimental.pallas import tpu as pltpu

F32 = jnp.float32
BF16 = jnp.bfloat16

HEAD_DIM = 64
LANES = 128
SUBLANES = 8
DILATION_PAIRS = ((128, 1), (512, 4), (2048, 16))
CONV_K = 31
FFN_CONV_K = 3
EPS = 1e-6
LOG2E = 1.4426950408889634
VMEM_LIMIT = 56 * 1024 * 1024

TM_INPROJ = 512
INPROJ_COLS = 512
TQ = 256
TS_OUT = 512
CONV_HALO = 32
CONV_ROWS = 64
TS_FFN = 512
FFN_HALO = 16
FFN_ROWS = 512
TF = 256


def _params(*sem):
    return pltpu.CompilerParams(dimension_semantics=sem, vmem_limit_bytes=VMEM_LIMIT)


def _sigmoid(x):
    return 1.0 / (1.0 + jnp.exp(-x))


def _inproj_kernel(x_ref, g_ref, wa_ref, wb_ref, qkv_ref, gl_ref, f_ref, h_scr, *, n_att, conv_ch):
    x = x_ref[...]
    ms = jnp.mean(x * x, axis=-1, keepdims=True)
    h_scr[...] = (x * lax.rsqrt(ms + EPS) * g_ref[...]).astype(BF16)
    h = h_scr[...]
    n_qkv = 3 * n_att
    n_glu = conv_ch // LANES
    glu = {}

    def emit(group, slab, r):
        if slab < n_qkv:
            if slab < n_att:
                r = r * (HEAD_DIM ** -0.5 * LOG2E)
            qkv_ref[group * n_qkv + slab] = r.astype(BF16)
        elif group == 0:
            f_ref[...] = r
        else:
            glu[slab - n_qkv] = r
            for j in range(n_glu):
                if j in glu and j + n_glu in glu:
                    gl_ref[:, j * LANES:(j + 1) * LANES] = (glu.pop(j) * _sigmoid(glu.pop(j + n_glu))).astype(BF16)

    for group, w_ref in enumerate((wa_ref, wb_ref)):
        n_cols = w_ref.shape[1]
        for c0 in range(0, n_cols, INPROJ_COLS):
            width = min(INPROJ_COLS, n_cols - c0)
            r = jnp.dot(h, w_ref[:, c0:c0 + width], preferred_element_type=F32)
            for j in range(width // LANES):
                emit(group, c0 // LANES + j, r[:, j * LANES:(j + 1) * LANES])


def _inproj(x, g, wa, wb, *, n_att, conv_ch):
    B, S, D = x.shape
    tm = TM_INPROJ
    kern = functools.partial(_inproj_kernel, n_att=n_att, conv_ch=conv_ch)
    return pl.pallas_call(
        kern,
        grid=(B, S // tm),
        in_specs=[
            pl.BlockSpec((None, tm, D), lambda b, i: (b, i, 0)),
            pl.BlockSpec((1, D), lambda b, i: (0, 0)),
            pl.BlockSpec(wa.shape, lambda b, i: (0, 0)),
            pl.BlockSpec(wb.shape, lambda b, i: (0, 0)),
        ],
        out_specs=[
            pl.BlockSpec((None, 6 * n_att, tm, LANES), lambda b, i: (b, 0, i, 0)),
            pl.BlockSpec((None, tm, conv_ch), lambda b, i: (b, i, 0)),
            pl.BlockSpec((None, tm, LANES), lambda b, i: (b, i, 0)),
        ],
        out_shape=[
            jax.ShapeDtypeStruct((B, 6 * n_att, S, LANES), BF16),
            jax.ShapeDtypeStruct((B, S, conv_ch), BF16),
            jax.ShapeDtypeStruct((B, S, LANES), F32),
        ],
        scratch_shapes=[pltpu.VMEM((tm, D), BF16)],
        compiler_params=_params("parallel", "parallel"),
        name="inproj",
    )(x, g, wa, wb)


def _cumsum_kernel(f_ref, b_ref, ccol_ref, crow_ref):
    S = f_ref.shape[0]
    ri = lax.broadcasted_iota(jnp.int32, (LANES, LANES), 0)
    ci = lax.broadcasted_iota(jnp.int32, (LANES, LANES), 1)
    tri = jnp.where(ci <= ri, 1.0, 0.0).astype(BF16)
    carry = jnp.zeros((1, LANES), F32)
    for blk in range(S // LANES):
        rows = slice(blk * LANES, (blk + 1) * LANES)
        z = f_ref[rows, :] + b_ref[...]
        a = jnp.minimum(z, 0.0) - jnp.log(1.0 + jnp.exp(-jnp.abs(z)))
        hi = a.astype(BF16)
        r1 = a - hi.astype(F32)
        mid = r1.astype(BF16)
        lo = (r1 - mid.astype(F32)).astype(BF16)
        c = (jnp.dot(tri, hi, preferred_element_type=F32)
             + jnp.dot(tri, mid, preferred_element_type=F32)
             + jnp.dot(tri, lo, preferred_element_type=F32)) + carry
        ccol_ref[rows, :] = c * LOG2E
        crow_ref[:, rows] = jnp.transpose(c * LOG2E)[0:SUBLANES, :]
        carry = c[LANES - 1:LANES, :]


def _cumsum(f, bias):
    B, S, _ = f.shape
    return pl.pallas_call(
        _cumsum_kernel,
        grid=(B,),
        in_specs=[
            pl.BlockSpec((None, S, LANES), lambda b: (b, 0, 0)),
            pl.BlockSpec((1, LANES), lambda b: (0, 0)),
        ],
        out_specs=[
            pl.BlockSpec((None, S, LANES), lambda b: (b, 0, 0)),
            pl.BlockSpec((None, SUBLANES, S), lambda b: (b, 0, 0)),
        ],
        out_shape=[
            jax.ShapeDtypeStruct((B, S, LANES), F32),
            jax.ShapeDtypeStruct((B, SUBLANES, S), F32),
        ],
        compiler_params=_params("parallel"),
        name="cumsum",
    )(f, bias)


_NT = (((1,), (1,)), ((), ()))


def _run_blocks(n, block_fn):
    def pair(j, carry):
        block_fn(2 * j)
        block_fn(2 * j + 1)
        return carry

    lax.fori_loop(0, lax.shift_right_logical(n, 1), pair, 0)

    @pl.when(jnp.bitwise_and(n, 1) == 1)
    def _():
        block_fn(n - 1)


def _build_value_operands(v_ref, vone_scr, qi, tq, n_pairs):
    lane = lax.broadcasted_iota(jnp.int32, (tq, LANES), 1)

    @pl.when(qi == 0)
    def _():
        def build(r, carry):
            r0 = pl.multiple_of(r * tq, tq)
            for pr in range(n_pairs):
                vf = v_ref[pr, pl.ds(r0, tq), :].astype(F32)
                vone_scr[2 * pr, pl.ds(r0, tq), :] = jnp.where(lane < HEAD_DIM, vf, 1.0).astype(BF16)
                vone_scr[2 * pr + 1, pl.ds(r0, tq), :] = jnp.where(lane >= HEAD_DIM, vf, 1.0).astype(BF16)
            return carry

        lax.fori_loop(0, v_ref.shape[1] // tq, build, 0)


def _attn_two_pass(k_block, o_ref, scr, qi, bias_fn, *, tq, n_pairs):
    qm_scr, vone_scr, s_scr, mx_scr, acc_scr = scr
    halves = tq // LANES
    lane = lax.broadcasted_iota(jnp.int32, (tq, LANES), 1)

    def score_block(kb, diag):
        k0 = pl.multiple_of(kb * tq, tq)
        for h in range(2 * n_pairs):
            s = lax.dot_general(qm_scr[h], k_block(h, k0), _NT, preferred_element_type=F32)
            cols = [bias_fn(h, c, s[:, c * LANES:(c + 1) * LANES], kb, diag) for c in range(halves)]
            s_scr[h, kb] = jnp.concatenate(cols, axis=1)
            blkmax = functools.reduce(jnp.maximum, cols)
            mx_scr[h] = blkmax if diag else jnp.maximum(mx_scr[h], blkmax)

    score_block(qi, True)
    _run_blocks(qi, lambda kb: score_block(kb, False))

    for h in range(2 * n_pairs):
        m = jnp.max(mx_scr[h], axis=-1, keepdims=True)
        mx_scr[h] = jnp.broadcast_to(m, (tq, LANES))
        acc_scr[h] = jnp.zeros((tq, LANES), F32)

    def value_block(kb):
        k0 = pl.multiple_of(kb * tq, tq)
        for h in range(2 * n_pairs):
            m = mx_scr[h]
            s = s_scr[h, kb]
            p = [jnp.exp2(s[:, c * LANES:(c + 1) * LANES] - m) for c in range(halves)]
            acc_scr[h] += jnp.dot(jnp.concatenate(p, axis=1).astype(BF16), vone_scr[h, pl.ds(k0, tq), :],
                                  preferred_element_type=F32)

    _run_blocks(qi + 1, value_block)

    for pr in range(n_pairs):
        a0, a1 = acc_scr[2 * pr], acc_scr[2 * pr + 1]
        num = jnp.where(lane < HEAD_DIM, a0, a1)
        den = pltpu.roll(jnp.where(lane < HEAD_DIM, a1, a0), HEAD_DIM, axis=1)
        o_ref[:, pr * LANES:(pr + 1) * LANES] = (num / den).astype(o_ref.dtype)


def _causal_mask(c, s, tq):
    ri = lax.broadcasted_iota(jnp.int32, (tq, LANES), 0)
    ci = lax.broadcasted_iota(jnp.int32, (tq, LANES), 1) + c * LANES
    return jnp.where(ci <= ri, s, -jnp.inf)


def _mask_head_queries(q_ref, qm_scr, tq, n_pairs):
    lane = lax.broadcasted_iota(jnp.int32, (tq, LANES), 1)
    for pr in range(n_pairs):
        qf = q_ref[pr].astype(F32)
        qm_scr[2 * pr] = jnp.where(lane < HEAD_DIM, qf, 0.0).astype(BF16)
        qm_scr[2 * pr + 1] = jnp.where(lane >= HEAD_DIM, qf, 0.0).astype(BF16)


def _fox_kernel(q_ref, k_ref, v_ref, ccol_ref, crow_ref, o_ref, cq_scr, *scr, tq, n_pairs):
    qi = pl.program_id(1)
    _build_value_operands(v_ref, scr[1], qi, tq, n_pairs)
    _mask_head_queries(q_ref, scr[0], tq, n_pairs)
    ccol_blk = ccol_ref[...]
    for h in range(2 * n_pairs):
        cq_scr[h] = jnp.broadcast_to(ccol_blk[:, h:h + 1], (tq, LANES))

    def bias_fn(h, c, s, kb, diag):
        k0 = pl.multiple_of(kb * tq, tq)
        c_k = crow_ref[h:h + 1, pl.ds(k0 + c * LANES, LANES)]
        s = (s + cq_scr[h]) - c_k
        return _causal_mask(c, s, tq) if diag else s

    k_block = lambda h, k0: k_ref[h // 2, pl.ds(k0, tq), :]
    _attn_two_pass(k_block, o_ref, scr, qi, bias_fn, tq=tq, n_pairs=n_pairs)


def _dil_kernel(q_ref, k_ref, v_ref, lw_ref, o_ref, *scr, tq, n_pairs):
    qi = pl.program_id(1)
    _build_value_operands(v_ref, scr[1], qi, tq, n_pairs)
    _mask_head_queries(q_ref, scr[0], tq, n_pairs)

    def bias_fn(h, c, s, kb, diag):
        off = 0 if diag else qi - kb
        return s + lw_ref[off, :, c * LANES:(c + 1) * LANES]

    k_block = lambda h, k0: k_ref[h // 2, pl.ds(k0, tq), :]
    _attn_two_pass(k_block, o_ref, scr, qi, bias_fn, tq=tq, n_pairs=n_pairs)


def _qkv_specs(S, tq, n_att, group):
    base = 3 * group
    return [
        pl.BlockSpec((None, n_att, tq, LANES), lambda b, qi: (b, base, qi, 0)),
        pl.BlockSpec((None, n_att, S, LANES), lambda b, qi: (b, base + 1, 0, 0)),
        pl.BlockSpec((None, n_att, S, LANES), lambda b, qi: (b, base + 2, 0, 0)),
    ]


def _attn_scratch(S, tq, n_att):
    n_heads = 2 * n_att
    head_tile = pltpu.VMEM((n_heads, tq, LANES), F32)
    return [
        pltpu.VMEM((n_heads, tq, LANES), BF16),
        pltpu.VMEM((n_heads, S, LANES), BF16),
        pltpu.VMEM((n_heads, S // tq, tq, tq), F32),
        head_tile, head_tile,
    ]


def _fox_attention(qkv, ccol, crow, *, n_att):
    B, _, S, _ = qkv.shape
    tq = TQ
    return pl.pallas_call(
        functools.partial(_fox_kernel, tq=tq, n_pairs=n_att),
        grid=(B, S // tq),
        in_specs=_qkv_specs(S, tq, n_att, 0) + [
            pl.BlockSpec((None, tq, LANES), lambda b, qi: (b, qi, 0)),
            pl.BlockSpec((None, SUBLANES, S), lambda b, qi: (b, 0, 0)),
        ],
        out_specs=pl.BlockSpec((None, tq, n_att * LANES), lambda b, qi: (b, qi, 0)),
        out_shape=jax.ShapeDtypeStruct((B, S, n_att * LANES), BF16),
        scratch_shapes=[pltpu.VMEM((2 * n_att, tq, LANES), F32)] + _attn_scratch(S, tq, n_att),
        compiler_params=_params("arbitrary", "arbitrary"),
        name="fox_attn",
    )(qkv, qkv, qkv, ccol, crow)


def _dil_log_weights(S, tq):
    for window, d in DILATION_PAIRS:
        assert S % ((window // d) * d) == 0, "dilated branches assume no sequence padding"
    nq = S // tq
    r = np.arange(tq)[:, None]
    c = np.arange(tq)[None, :]
    tabs = []
    for off in range(nq):
        delta = off * tq + r - c
        w = np.zeros((tq, tq), np.int64)
        for window, d in DILATION_PAIRS:
            w += (delta >= 0) & (delta % d == 0) & (delta // d <= window // d)
        with np.errstate(divide="ignore"):
            tabs.append(np.log2(w.astype(np.float64)))
    return jnp.asarray(np.stack(tabs).astype(np.float32))


def _dil_attention(qkv, logw, *, n_att):
    B, _, S, _ = qkv.shape
    tq = TQ
    nq = S // tq
    return pl.pallas_call(
        functools.partial(_dil_kernel, tq=tq, n_pairs=n_att),
        grid=(B, nq),
        in_specs=_qkv_specs(S, tq, n_att, 1) + [
            pl.BlockSpec((nq, tq, tq), lambda b, qi: (0, 0, 0)),
        ],
        out_specs=pl.BlockSpec((None, tq, n_att * LANES), lambda b, qi: (b, qi, 0)),
        out_shape=jax.ShapeDtypeStruct((B, S, n_att * LANES), BF16),
        scratch_shapes=_attn_scratch(S, tq, n_att),
        compiler_params=_params("arbitrary", "arbitrary"),
        name="dil_attn",
    )(qkv, qkv, qkv, logw)


def _shift_matrices():
    k = CONV_ROWS + CONV_HALO
    s = np.zeros((SUBLANES, k, k), np.float32)
    for r in range(SUBLANES):
        s[r, np.arange(k - r), np.arange(k - r) + r] = 1.0
    return jnp.asarray(s, dtype=BF16)


def _outproj_kernel(x_ref, oa_ref, ob_ref, gl_ref, glh_ref, sh_ref, gfox_ref, gdil_ref, cw_ref, cb_ref,
                    cng_ref, cnb_ref, wo_ref, o_ref, ext_scr, y_scr, *, ts, w_att, conv_ch):
    i = pl.program_id(1)
    ext_scr[0:CONV_HALO, :] = jnp.where(i > 0, glh_ref[...], jnp.zeros_like(glh_ref))
    ext_scr[CONV_HALO:, :] = gl_ref[...]
    first = CONV_HALO - (CONV_K - 1)
    for c in range(ts // CONV_ROWS):
        r0 = c * CONV_ROWS
        win = ext_scr[r0:r0 + CONV_ROWS + CONV_HALO, :]
        acc = jnp.broadcast_to(cb_ref[...], (CONV_ROWS, conv_ch))
        for r in range(SUBLANES):
            shifted = jnp.dot(sh_ref[r], win, preferred_element_type=F32)
            for k in range(CONV_K):
                e = first + k
                if e % SUBLANES == r:
                    a0 = e - r
                    acc = acc + cw_ref[k:k + 1, :] * shifted[a0:a0 + CONV_ROWS, :]
        mu = jnp.mean(acc, axis=-1, keepdims=True)
        xc = acc - mu
        yn = xc * lax.rsqrt(jnp.mean(xc * xc, axis=-1, keepdims=True) + EPS)
        yn = yn * cng_ref[...] + cnb_ref[...]
        y_scr[r0:r0 + CONV_ROWS, 2 * w_att:] = (yn * _sigmoid(yn)).astype(BF16)
    for o_in, g_in, lo in ((oa_ref, gfox_ref, 0), (ob_ref, gdil_ref, w_att)):
        a = o_in[...].astype(F32)
        ms = jnp.mean(a * a, axis=-1, keepdims=True)
        y_scr[:, lo:lo + w_att] = (a * lax.rsqrt(ms + EPS) * g_in[...]).astype(BF16)
    o_ref[...] = x_ref[...] + jnp.dot(y_scr[...], wo_ref[...], preferred_element_type=F32)


def _outproj(x, oa, ob, gl, g_fox, g_dil, cw, cb, cng, cnb, wo):
    B, S, D = x.shape
    ts = TS_OUT
    w_att = oa.shape[-1]
    conv_ch = gl.shape[-1]
    hb = ts // CONV_HALO
    shifts = _shift_matrices()
    kern = functools.partial(_outproj_kernel, ts=ts, w_att=w_att, conv_ch=conv_ch)
    vec = lambda n: pl.BlockSpec((1, n), lambda b, i: (0, 0))
    return pl.pallas_call(
        kern,
        grid=(B, S // ts),
        in_specs=[
            pl.BlockSpec((None, ts, D), lambda b, i: (b, i, 0)),
            pl.BlockSpec((None, ts, w_att), lambda b, i: (b, i, 0)),
            pl.BlockSpec((None, ts, w_att), lambda b, i: (b, i, 0)),
            pl.BlockSpec((None, ts, conv_ch), lambda b, i: (b, i, 0)),
            pl.BlockSpec((None, CONV_HALO, conv_ch), lambda b, i: (b, jnp.maximum(i * hb - 1, 0), 0)),
            pl.BlockSpec(shifts.shape, lambda b, i: (0, 0, 0)),
            vec(w_att), vec(w_att),
            pl.BlockSpec((CONV_K, conv_ch), lambda b, i: (0, 0)),
            vec(conv_ch), vec(conv_ch), vec(conv_ch),
            pl.BlockSpec((D, D), lambda b, i: (0, 0)),
        ],
        out_specs=pl.BlockSpec((None, ts, D), lambda b, i: (b, i, 0)),
        out_shape=jax.ShapeDtypeStruct((B, S, D), F32),
        scratch_shapes=[pltpu.VMEM((ts + CONV_HALO, conv_ch), BF16), pltpu.VMEM((ts, D), BF16)],
        compiler_params=_params("parallel", "parallel"),
        name="outproj",
    )(x, oa, ob, gl, gl, shifts, g_fox, g_dil, cw, cb, cng, cnb, wo)


def _ffn_kernel(x_ref, xh_ref, g_ref, wup_ref, cw_ref, cb_ref, wdn_ref, gf_ref, o_ref,
                h_scr, hid_scr, *, ts, d_ff, final):
    i = pl.program_id(1)
    n_chunks = d_ff // TF

    def norm(v):
        return (v * lax.rsqrt(jnp.mean(v * v, axis=-1, keepdims=True) + EPS) * g_ref[...]).astype(BF16)

    h_scr[0:FFN_HALO, :] = norm(jnp.where(i > 0, xh_ref[...], 0.0))
    h_scr[FFN_HALO:, :] = norm(x_ref[...])
    first = FFN_HALO - (FFN_CONV_K - 1)

    def cols(c, part):
        return pl.ds(pl.multiple_of(part * d_ff + c * TF, TF), TF)

    def chunk(c):
        for r0 in range(0, ts, FFN_ROWS):
            hb = h_scr[r0:r0 + FFN_ROWS + FFN_HALO, :]
            halves = []
            for part in range(2):
                u = jnp.dot(hb, wup_ref[:, cols(c, part)], preferred_element_type=F32)
                w = cw_ref[:, cols(c, part)]
                uc = cb_ref[:, cols(c, part)]
                for k in range(FFN_CONV_K):
                    uc = uc + w[k:k + 1, :] * u[first + k:first + k + FFN_ROWS, :]
                halves.append(uc)
            gate, val = halves
            hid_scr[c, r0:r0 + FFN_ROWS, :] = (gate * _sigmoid(gate) * val).astype(BF16)

    def pair(j, carry):
        chunk(2 * j)
        chunk(2 * j + 1)
        return carry

    lax.fori_loop(0, n_chunks // 2, pair, 0)
    if n_chunks % 2:
        chunk(n_chunks - 1)

    acc = jnp.dot(hid_scr[0], wdn_ref[0:TF, :], preferred_element_type=F32)
    for c in range(1, n_chunks):
        acc = acc + jnp.dot(hid_scr[c], wdn_ref[c * TF:(c + 1) * TF, :], preferred_element_type=F32)
    out = x_ref[...] + acc
    if final:
        out = out * lax.rsqrt(jnp.mean(out * out, axis=-1, keepdims=True) + EPS) * gf_ref[...]
    o_ref[...] = out


def _ffn(x, g, wup, cw, cb, wdn, g_final, *, final):
    B, S, D = x.shape
    ts = TS_FFN
    d_ff = wdn.shape[0]
    hb = ts // FFN_HALO
    kern = functools.partial(_ffn_kernel, ts=ts, d_ff=d_ff, final=final)
    const = lambda shape: pl.BlockSpec(shape, lambda b, i: (0,) * len(shape))
    return pl.pallas_call(
        kern,
        grid=(B, S // ts),
        in_specs=[
            pl.BlockSpec((None, ts, D), lambda b, i: (b, i, 0)),
            pl.BlockSpec((None, FFN_HALO, D), lambda b, i: (b, jnp.maximum(i * hb - 1, 0), 0)),
            const((1, D)),
            const((D, 2 * d_ff)),
            const((FFN_CONV_K, 2 * d_ff)),
            const((1, 2 * d_ff)),
            const((d_ff, D)),
            const((1, D)),
        ],
        out_specs=pl.BlockSpec((None, ts, D), lambda b, i: (b, i, 0)),
        out_shape=jax.ShapeDtypeStruct((B, S, D), F32),
        scratch_shapes=[
            pltpu.VMEM((ts + FFN_HALO, D), BF16),
            pltpu.VMEM((d_ff // TF, ts, TF), BF16),
        ],
        compiler_params=_params("parallel", "parallel"),
        name="ffn_final" if final else "ffn",
    )(x, x, g, wup, cw, cb, wdn, g_final)


def kernel(x, ln1_g, w_in, b_forget, g_out_fox, g_out_dil, conv_w, conv_b, cnorm_g, cnorm_b,
           w_o, ln2_g, w_up, ffn_conv_w, ffn_conv_b, w_down, g_final):
    B, S, D = x.shape
    depth = w_in.shape[0]
    w_fox = g_out_fox.shape[-1]
    w_dil = g_out_dil.shape[-1]
    conv_ch = conv_w.shape[-1]
    n_heads_fox = b_forget.shape[-1]
    d_ff = w_down.shape[1]
    assert w_fox == w_dil and w_fox % LANES == 0 and n_heads_fox == w_fox // HEAD_DIM
    assert n_heads_fox <= SUBLANES and d_ff % TF == 0 and D % LANES == 0
    assert conv_ch % LANES == 0 and S % TM_INPROJ == 0 and S % TQ == 0 and S % TS_OUT == 0 and S % TS_FFN == 0
    assert CONV_K - 1 <= CONV_HALO and TS_OUT % CONV_ROWS == 0 and CONV_ROWS % CONV_HALO == 0
    n_att = w_fox // LANES
    off_fa = 3 * w_fox
    off_qb = off_fa + n_heads_fox
    off_gv = off_qb + 3 * w_dil
    assert w_in.shape[-1] == off_gv + 2 * conv_ch

    logw = _dil_log_weights(S, TQ)
    row = lambda v: v.reshape(1, -1)

    for l in range(depth):
        w_f = jnp.pad(w_in[l, :, off_fa:off_qb], ((0, 0), (0, LANES - n_heads_fox)))
        w_a = jnp.concatenate([w_in[l, :, :off_fa], w_f], axis=1).astype(BF16)
        w_b = w_in[l, :, off_qb:].astype(BF16)
        b_f = jnp.pad(b_forget[l], (0, LANES - n_heads_fox)).reshape(1, LANES)

        qkv, gl, f = _inproj(x, row(ln1_g[l]), w_a, w_b, n_att=n_att, conv_ch=conv_ch)
        ccol, crow = _cumsum(f, b_f)
        oa = _fox_attention(qkv, ccol, crow, n_att=n_att)
        ob = _dil_attention(qkv, logw, n_att=n_att)
        x = _outproj(x, oa, ob, gl, row(g_out_fox[l]), row(g_out_dil[l]), conv_w[l], row(conv_b[l]),
                     row(cnorm_g[l]), row(cnorm_b[l]), w_o[l].astype(BF16))
        x = _ffn(x, row(ln2_g[l]), w_up[l].astype(BF16), ffn_conv_w[l], row(ffn_conv_b[l]),
                 w_down[l].astype(BF16), row(g_final), final=(l == depth - 1))
    return x
```

```python
import functools

import numpy as np
import jax
import jax.numpy as jnp
from jax import lax
from jax.experimental import pallas as pl
from jax.experimental.pallas import tpu as pltpu

F32 = jnp.float32
BF16 = jnp.bfloat16

HEAD_DIM = 64
LANES = 128
SUBLANES = 8
DILATION_PAIRS = ((128, 1), (512, 4), (2048, 16))
CONV_K = 31
FFN_CONV_K = 3
EPS = 1e-6
LOG2E = 1.4426950408889634
VMEM_LIMIT = 56 * 1024 * 1024

TM_INPROJ = 1024
INPROJ_COLS = 512
TQ = 256
TS_OUT = 512
CONV_HALO = 32
CONV_ROWS = 64
TS_FFN = 1024
FFN_HALO = 16
FFN_ROWS = 1024
TF = 256


def _params(*sem):
    return pltpu.CompilerParams(dimension_semantics=sem, vmem_limit_bytes=VMEM_LIMIT)


def _sigmoid(x):
    return 1.0 / (1.0 + jnp.exp(-x))


def _inproj_kernel(x_ref, g_ref, wa_ref, wb_ref, qkv_ref, gl_ref, f_ref, h_scr, *, n_att, conv_ch):
    x = x_ref[...]
    ms = jnp.mean(x * x, axis=-1, keepdims=True)
    h_scr[...] = (x * lax.rsqrt(ms + EPS) * g_ref[...]).astype(BF16)
    h = h_scr[...]
    n_qkv = 3 * n_att
    n_glu = conv_ch // LANES
    glu = {}

    def emit(group, slab, r):
        if slab < n_qkv:
            if slab < n_att:
                r = r * (HEAD_DIM ** -0.5 * LOG2E)
            qkv_ref[group * n_qkv + slab] = r.astype(BF16)
        elif group == 0:
            f_ref[...] = r
        else:
            glu[slab - n_qkv] = r
            for j in range(n_glu):
                if j in glu and j + n_glu in glu:
                    gl_ref[:, j * LANES:(j + 1) * LANES] = (glu.pop(j) * _sigmoid(glu.pop(j + n_glu))).astype(BF16)

    for group, w_ref in enumerate((wa_ref, wb_ref)):
        n_cols = w_ref.shape[1]
        for c0 in range(0, n_cols, INPROJ_COLS):
            width = min(INPROJ_COLS, n_cols - c0)
            r = jnp.dot(h, w_ref[:, c0:c0 + width], preferred_element_type=F32)
            for j in range(width // LANES):
                emit(group, c0 // LANES + j, r[:, j * LANES:(j + 1) * LANES])


def _inproj(x, g, wa, wb, *, n_att, conv_ch):
    B, S, D = x.shape
    tm = TM_INPROJ
    kern = functools.partial(_inproj_kernel, n_att=n_att, conv_ch=conv_ch)
    return pl.pallas_call(
        kern,
        grid=(B, S // tm),
        in_specs=[
            pl.BlockSpec((None, tm, D), lambda b, i: (b, i, 0)),
            pl.BlockSpec((1, D), lambda b, i: (0, 0)),
            pl.BlockSpec(wa.shape, lambda b, i: (0, 0)),
            pl.BlockSpec(wb.shape, lambda b, i: (0, 0)),
        ],
        out_specs=[
            pl.BlockSpec((None, 6 * n_att, tm, LANES), lambda b, i: (b, 0, i, 0)),
            pl.BlockSpec((None, tm, conv_ch), lambda b, i: (b, i, 0)),
            pl.BlockSpec((None, tm, LANES), lambda b, i: (b, i, 0)),
        ],
        out_shape=[
            jax.ShapeDtypeStruct((B, 6 * n_att, S, LANES), BF16),
            jax.ShapeDtypeStruct((B, S, conv_ch), BF16),
            jax.ShapeDtypeStruct((B, S, LANES), F32),
        ],
        scratch_shapes=[pltpu.VMEM((tm, D), BF16)],
        compiler_params=_params("parallel", "parallel"),
        name="inproj",
    )(x, g, wa, wb)


def _cumsum_kernel(f_ref, b_ref, ccol_ref, crow_ref):
    S = f_ref.shape[0]
    ri = lax.broadcasted_iota(jnp.int32, (LANES, LANES), 0)
    ci = lax.broadcasted_iota(jnp.int32, (LANES, LANES), 1)
    tri = jnp.where(ci <= ri, 1.0, 0.0).astype(BF16)
    carry = jnp.zeros((1, LANES), F32)
    for blk in range(S // LANES):
        rows = slice(blk * LANES, (blk + 1) * LANES)
        z = f_ref[rows, :] + b_ref[...]
        a = jnp.minimum(z, 0.0) - jnp.log(1.0 + jnp.exp(-jnp.abs(z)))
        hi = a.astype(BF16)
        r1 = a - hi.astype(F32)
        mid = r1.astype(BF16)
        lo = (r1 - mid.astype(F32)).astype(BF16)
        c = (jnp.dot(tri, hi, preferred_element_type=F32)
             + jnp.dot(tri, mid, preferred_element_type=F32)
             + jnp.dot(tri, lo, preferred_element_type=F32)) + carry
        ccol_ref[rows, :] = c * LOG2E
        crow_ref[:, rows] = jnp.transpose(c * LOG2E)[0:SUBLANES, :]
        carry = c[LANES - 1:LANES, :]


def _cumsum(f, bias):
    B, S, _ = f.shape
    return pl.pallas_call(
        _cumsum_kernel,
        grid=(B,),
        in_specs=[
            pl.BlockSpec((None, S, LANES), lambda b: (b, 0, 0)),
            pl.BlockSpec((1, LANES), lambda b: (0, 0)),
        ],
        out_specs=[
            pl.BlockSpec((None, S, LANES), lambda b: (b, 0, 0)),
            pl.BlockSpec((None, SUBLANES, S), lambda b: (b, 0, 0)),
        ],
        out_shape=[
            jax.ShapeDtypeStruct((B, S, LANES), F32),
            jax.ShapeDtypeStruct((B, SUBLANES, S), F32),
        ],
        compiler_params=_params("parallel"),
        name="cumsum",
    )(f, bias)


_NT = (((1,), (1,)), ((), ()))


def _run_blocks(n, block_fn):
    def quad(j, carry):
        for u in range(4):
            block_fn(4 * j + u)
        return carry

    lax.fori_loop(0, lax.shift_right_logical(n, 2), quad, 0)
    done = jnp.bitwise_and(n, -4)

    @pl.when(jnp.bitwise_and(n, 2) == 2)
    def _():
        block_fn(done)
        block_fn(done + 1)

    @pl.when(jnp.bitwise_and(n, 1) == 1)
    def _():
        block_fn(n - 1)


def _build_value_operands(v_ref, vone_scr, qi, tq, n_pairs):
    lane = lax.broadcasted_iota(jnp.int32, (tq, LANES), 1)

    @pl.when(qi == 0)
    def _():
        def build(r, carry):
            r0 = pl.multiple_of(r * tq, tq)
            for pr in range(n_pairs):
                vf = v_ref[pr, pl.ds(r0, tq), :].astype(F32)
                vone_scr[2 * pr, pl.ds(r0, tq), :] = jnp.where(lane < HEAD_DIM, vf, 1.0).astype(BF16)
                vone_scr[2 * pr + 1, pl.ds(r0, tq), :] = jnp.where(lane >= HEAD_DIM, vf, 1.0).astype(BF16)
            return carry

        lax.fori_loop(0, v_ref.shape[1] // tq, build, 0)


def _attn_two_pass(k_block, o_ref, scr, qi, bias_fn, *, tq, n_pairs):
    qm_scr, vone_scr, s_scr, mx_scr, acc_scr = scr
    halves = tq // LANES
    lane = lax.broadcasted_iota(jnp.int32, (tq, LANES), 1)

    def score_block(kb, diag):
        k0 = pl.multiple_of(kb * tq, tq)
        for h in range(2 * n_pairs):
            s = lax.dot_general(qm_scr[h], k_block(h, k0), _NT, preferred_element_type=F32)
            cols = [bias_fn(h, c, s[:, c * LANES:(c + 1) * LANES], kb, diag) for c in range(halves)]
            s_scr[h, kb] = jnp.concatenate(cols, axis=1)
            blkmax = functools.reduce(jnp.maximum, cols)
            mx_scr[h] = blkmax if diag else jnp.maximum(mx_scr[h], blkmax)

    score_block(qi, True)
    _run_blocks(qi, lambda kb: score_block(kb, False))

    for h in range(2 * n_pairs):
        m = jnp.max(mx_scr[h], axis=-1, keepdims=True)
        mx_scr[h] = jnp.broadcast_to(m, (tq, LANES))
        acc_scr[h] = jnp.zeros((tq, LANES), F32)

    def value_block(kb):
        k0 = pl.multiple_of(kb * tq, tq)
        for h in range(2 * n_pairs):
            m = mx_scr[h]
            s = s_scr[h, kb]
            p = [jnp.exp2(s[:, c * LANES:(c + 1) * LANES] - m) for c in range(halves)]
            acc_scr[h] += jnp.dot(jnp.concatenate(p, axis=1).astype(BF16), vone_scr[h, pl.ds(k0, tq), :],
                                  preferred_element_type=F32)

    _run_blocks(qi + 1, value_block)

    for pr in range(n_pairs):
        a0, a1 = acc_scr[2 * pr], acc_scr[2 * pr + 1]
        num = jnp.where(lane < HEAD_DIM, a0, a1)
        den = pltpu.roll(jnp.where(lane < HEAD_DIM, a1, a0), HEAD_DIM, axis=1)
        o_ref[:, pr * LANES:(pr + 1) * LANES] = (num / den).astype(o_ref.dtype)


def _causal_mask(c, s, tq):
    ri = lax.broadcasted_iota(jnp.int32, (tq, LANES), 0)
    ci = lax.broadcasted_iota(jnp.int32, (tq, LANES), 1) + c * LANES
    return jnp.where(ci <= ri, s, -jnp.inf)


def _mask_head_queries(q_ref, qm_scr, tq, n_pairs):
    lane = lax.broadcasted_iota(jnp.int32, (tq, LANES), 1)
    for pr in range(n_pairs):
        qf = q_ref[pr].astype(F32)
        qm_scr[2 * pr] = jnp.where(lane < HEAD_DIM, qf, 0.0).astype(BF16)
        qm_scr[2 * pr + 1] = jnp.where(lane >= HEAD_DIM, qf, 0.0).astype(BF16)


def _fox_kernel(q_ref, k_ref, v_ref, ccol_ref, crow_ref, o_ref, cq_scr, *scr, tq, n_pairs):
    qi = pl.program_id(1)
    _build_value_operands(v_ref, scr[1], qi, tq, n_pairs)
    _mask_head_queries(q_ref, scr[0], tq, n_pairs)
    ccol_blk = ccol_ref[...]
    for h in range(2 * n_pairs):
        cq_scr[h] = jnp.broadcast_to(ccol_blk[:, h:h + 1], (tq, LANES))

    def bias_fn(h, c, s, kb, diag):
        k0 = pl.multiple_of(kb * tq, tq)
        c_k = crow_ref[h:h + 1, pl.ds(k0 + c * LANES, LANES)]
        s = (s + cq_scr[h]) - c_k
        return _causal_mask(c, s, tq) if diag else s

    k_block = lambda h, k0: k_ref[h // 2, pl.ds(k0, tq), :]
    _attn_two_pass(k_block, o_ref, scr, qi, bias_fn, tq=tq, n_pairs=n_pairs)


def _dil_kernel(q_ref, k_ref, v_ref, lw_ref, o_ref, *scr, tq, n_pairs):
    qi = pl.program_id(1)
    _build_value_operands(v_ref, scr[1], qi, tq, n_pairs)
    _mask_head_queries(q_ref, scr[0], tq, n_pairs)

    def bias_fn(h, c, s, kb, diag):
        off = 0 if diag else qi - kb
        return s + lw_ref[off, :, c * LANES:(c + 1) * LANES]

    k_block = lambda h, k0: k_ref[h // 2, pl.ds(k0, tq), :]
    _attn_two_pass(k_block, o_ref, scr, qi, bias_fn, tq=tq, n_pairs=n_pairs)


def _qkv_specs(S, tq, n_att, group):
    base = 3 * group
    return [
        pl.BlockSpec((None, n_att, tq, LANES), lambda b, qi: (b, base, qi, 0)),
        pl.BlockSpec((None, n_att, S, LANES), lambda b, qi: (b, base + 1, 0, 0)),
        pl.BlockSpec((None, n_att, S, LANES), lambda b, qi: (b, base + 2, 0, 0)),
    ]


def _attn_scratch(S, tq, n_att):
    n_heads = 2 * n_att
    head_tile = pltpu.VMEM((n_heads, tq, LANES), F32)
    return [
        pltpu.VMEM((n_heads, tq, LANES), BF16),
        pltpu.VMEM((n_heads, S, LANES), BF16),
        pltpu.VMEM((n_heads, S // tq, tq, tq), F32),
        head_tile, head_tile,
    ]


def _fox_attention(qkv, ccol, crow, *, n_att):
    B, _, S, _ = qkv.shape
    tq = TQ
    return pl.pallas_call(
        functools.partial(_fox_kernel, tq=tq, n_pairs=n_att),
        grid=(B, S // tq),
        in_specs=_qkv_specs(S, tq, n_att, 0) + [
            pl.BlockSpec((None, tq, LANES), lambda b, qi: (b, qi, 0)),
            pl.BlockSpec((None, SUBLANES, S), lambda b, qi: (b, 0, 0)),
        ],
        out_specs=pl.BlockSpec((None, tq, n_att * LANES), lambda b, qi: (b, qi, 0)),
        out_shape=jax.ShapeDtypeStruct((B, S, n_att * LANES), BF16),
        scratch_shapes=[pltpu.VMEM((2 * n_att, tq, LANES), F32)] + _attn_scratch(S, tq, n_att),
        compiler_params=_params("arbitrary", "arbitrary"),
        name="fox_attn",
    )(qkv, qkv, qkv, ccol, crow)


def _dil_log_weights(S, tq):
    for window, d in DILATION_PAIRS:
        assert S % ((window // d) * d) == 0, "dilated branches assume no sequence padding"
    nq = S // tq
    r = np.arange(tq)[:, None]
    c = np.arange(tq)[None, :]
    tabs = []
    for off in range(nq):
        delta = off * tq + r - c
        w = np.zeros((tq, tq), np.int64)
        for window, d in DILATION_PAIRS:
            w += (delta >= 0) & (delta % d == 0) & (delta // d <= window // d)
        with np.errstate(divide="ignore"):
            tabs.append(np.log2(w.astype(np.float64)))
    return jnp.asarray(np.stack(tabs).astype(np.float32))


def _dil_attention(qkv, logw, *, n_att):
    B, _, S, _ = qkv.shape
    tq = TQ
    nq = S // tq
    return pl.pallas_call(
        functools.partial(_dil_kernel, tq=tq, n_pairs=n_att),
        grid=(B, nq),
        in_specs=_qkv_specs(S, tq, n_att, 1) + [
            pl.BlockSpec((nq, tq, tq), lambda b, qi: (0, 0, 0)),
        ],
        out_specs=pl.BlockSpec((None, tq, n_att * LANES), lambda b, qi: (b, qi, 0)),
        out_shape=jax.ShapeDtypeStruct((B, S, n_att * LANES), BF16),
        scratch_shapes=_attn_scratch(S, tq, n_att),
        compiler_params=_params("arbitrary", "arbitrary"),
        name="dil_attn",
    )(qkv, qkv, qkv, logw)


def _shift_matrices():
    k = CONV_ROWS + CONV_HALO
    s = np.zeros((SUBLANES, k, k), np.float32)
    for r in range(SUBLANES):
        s[r, np.arange(k - r), np.arange(k - r) + r] = 1.0
    return jnp.asarray(s, dtype=BF16)


def _outproj_kernel(x_ref, oa_ref, ob_ref, gl_ref, glh_ref, sh_ref, gfox_ref, gdil_ref, cw_ref, cb_ref,
                    cng_ref, cnb_ref, wo_ref, o_ref, ext_scr, y_scr, *, ts, w_att, conv_ch):
    i = pl.program_id(1)
    ext_scr[0:CONV_HALO, :] = jnp.where(i > 0, glh_ref[...], jnp.zeros_like(glh_ref))
    ext_scr[CONV_HALO:, :] = gl_ref[...]
    first = CONV_HALO - (CONV_K - 1)
    for c in range(ts // CONV_ROWS):
        r0 = c * CONV_ROWS
        win = ext_scr[r0:r0 + CONV_ROWS + CONV_HALO, :]
        acc = jnp.broadcast_to(cb_ref[...], (CONV_ROWS, conv_ch))
        for r in range(SUBLANES):
            shifted = jnp.dot(sh_ref[r], win, preferred_element_type=F32)
            for k in range(CONV_K):
                e = first + k
                if e % SUBLANES == r:
                    a0 = e - r
                    acc = acc + cw_ref[k:k + 1, :] * shifted[a0:a0 + CONV_ROWS, :]
        mu = jnp.mean(acc, axis=-1, keepdims=True)
        xc = acc - mu
        yn = xc * lax.rsqrt(jnp.mean(xc * xc, axis=-1, keepdims=True) + EPS)
        yn = yn * cng_ref[...] + cnb_ref[...]
        y_scr[r0:r0 + CONV_ROWS, 2 * w_att:] = (yn * _sigmoid(yn)).astype(BF16)
    for o_in, g_in, lo in ((oa_ref, gfox_ref, 0), (ob_ref, gdil_ref, w_att)):
        a = o_in[...].astype(F32)
        ms = jnp.mean(a * a, axis=-1, keepdims=True)
        y_scr[:, lo:lo + w_att] = (a * lax.rsqrt(ms + EPS) * g_in[...]).astype(BF16)
    o_ref[...] = x_ref[...] + jnp.dot(y_scr[...], wo_ref[...], preferred_element_type=F32)


def _outproj(x, oa, ob, gl, g_fox, g_dil, cw, cb, cng, cnb, wo):
    B, S, D = x.shape
    ts = TS_OUT
    w_att = oa.shape[-1]
    conv_ch = gl.shape[-1]
    hb = ts // CONV_HALO
    shifts = _shift_matrices()
    kern = functools.partial(_outproj_kernel, ts=ts, w_att=w_att, conv_ch=conv_ch)
    vec = lambda n: pl.BlockSpec((1, n), lambda b, i: (0, 0))
    return pl.pallas_call(
        kern,
        grid=(B, S // ts),
        in_specs=[
            pl.BlockSpec((None, ts, D), lambda b, i: (b, i, 0)),
            pl.BlockSpec((None, ts, w_att), lambda b, i: (b, i, 0)),
            pl.BlockSpec((None, ts, w_att), lambda b, i: (b, i, 0)),
            pl.BlockSpec((None, ts, conv_ch), lambda b, i: (b, i, 0)),
            pl.BlockSpec((None, CONV_HALO, conv_ch), lambda b, i: (b, jnp.maximum(i * hb - 1, 0), 0)),
            pl.BlockSpec(shifts.shape, lambda b, i: (0, 0, 0)),
            vec(w_att), vec(w_att),
            pl.BlockSpec((CONV_K, conv_ch), lambda b, i: (0, 0)),
            vec(conv_ch), vec(conv_ch), vec(conv_ch),
            pl.BlockSpec((D, D), lambda b, i: (0, 0)),
        ],
        out_specs=pl.BlockSpec((None, ts, D), lambda b, i: (b, i, 0)),
        out_shape=jax.ShapeDtypeStruct((B, S, D), F32),
        scratch_shapes=[pltpu.VMEM((ts + CONV_HALO, conv_ch), BF16), pltpu.VMEM((ts, D), BF16)],
        compiler_params=_params("parallel", "parallel"),
        name="outproj",
    )(x, oa, ob, gl, gl, shifts, g_fox, g_dil, cw, cb, cng, cnb, wo)


def _ffn_kernel(x_ref, xh_ref, g_ref, wup_ref, cw_ref, cb_ref, wdn_ref, gf_ref, o_ref,
                h_scr, hid_scr, *, ts, d_ff, final):
    i = pl.program_id(1)
    n_chunks = d_ff // TF

    def norm(v):
        return (v * lax.rsqrt(jnp.mean(v * v, axis=-1, keepdims=True) + EPS) * g_ref[...]).astype(BF16)

    h_scr[0:FFN_HALO, :] = norm(jnp.where(i > 0, xh_ref[...], 0.0))
    h_scr[FFN_HALO:, :] = norm(x_ref[...])
    first = FFN_HALO - (FFN_CONV_K - 1)

    def cols(c, part):
        return pl.ds(pl.multiple_of(part * d_ff + c * TF, TF), TF)

    def chunk(c):
        for r0 in range(0, ts, FFN_ROWS):
            hb = h_scr[r0:r0 + FFN_ROWS + FFN_HALO, :]
            halves = []
            for part in range(2):
                u = jnp.dot(hb, wup_ref[:, cols(c, part)], preferred_element_type=F32)
                w = cw_ref[:, cols(c, part)]
                uc = cb_ref[:, cols(c, part)]
                for k in range(FFN_CONV_K):
                    uc = uc + w[k:k + 1, :] * u[first + k:first + k + FFN_ROWS, :]
                halves.append(uc)
            gate, val = halves
            hid_scr[c, r0:r0 + FFN_ROWS, :] = (gate * _sigmoid(gate) * val).astype(BF16)

    def pair(j, carry):
        chunk(2 * j)
        chunk(2 * j + 1)
        return carry

    lax.fori_loop(0, n_chunks // 2, pair, 0)
    if n_chunks % 2:
        chunk(n_chunks - 1)

    acc = jnp.dot(hid_scr[0], wdn_ref[0:TF, :], preferred_element_type=F32)
    for c in range(1, n_chunks):
        acc = acc + jnp.dot(hid_scr[c], wdn_ref[c * TF:(c + 1) * TF, :], preferred_element_type=F32)
    out = x_ref[...] + acc
    if final:
        out = out * lax.rsqrt(jnp.mean(out * out, axis=-1, keepdims=True) + EPS) * gf_ref[...]
    o_ref[...] = out


def _ffn(x, g, wup, cw, cb, wdn, g_final, *, final):
    B, S, D = x.shape
    ts = TS_FFN
    d_ff = wdn.shape[0]
    hb = ts // FFN_HALO
    kern = functools.partial(_ffn_kernel, ts=ts, d_ff=d_ff, final=final)
    const = lambda shape: pl.BlockSpec(shape, lambda b, i: (0,) * len(shape), pipeline_mode=pl.Buffered(1))
    return pl.pallas_call(
        kern,
        grid=(B, S // ts),
        in_specs=[
            pl.BlockSpec((None, ts, D), lambda b, i: (b, i, 0)),
            pl.BlockSpec((None, FFN_HALO, D), lambda b, i: (b, jnp.maximum(i * hb - 1, 0), 0)),
            const((1, D)),
            const((D, 2 * d_ff)),
            const((FFN_CONV_K, 2 * d_ff)),
            const((1, 2 * d_ff)),
            const((d_ff, D)),
            const((1, D)),
        ],
        out_specs=pl.BlockSpec((None, ts, D), lambda b, i: (b, i, 0)),
        out_shape=jax.ShapeDtypeStruct((B, S, D), F32),
        scratch_shapes=[
            pltpu.VMEM((ts + FFN_HALO, D), BF16),
            pltpu.VMEM((d_ff // TF, ts, TF), BF16),
        ],
        compiler_params=_params("parallel", "parallel"),
        name="ffn_final" if final else "ffn",
    )(x, x, g, wup, cw, cb, wdn, g_final)


def kernel(x, ln1_g, w_in, b_forget, g_out_fox, g_out_dil, conv_w, conv_b, cnorm_g, cnorm_b,
           w_o, ln2_g, w_up, ffn_conv_w, ffn_conv_b, w_down, g_final):
    B, S, D = x.shape
    depth = w_in.shape[0]
    w_fox = g_out_fox.shape[-1]
    w_dil = g_out_dil.shape[-1]
    conv_ch = conv_w.shape[-1]
    n_heads_fox = b_forget.shape[-1]
    d_ff = w_down.shape[1]
    assert w_fox == w_dil and w_fox % LANES == 0 and n_heads_fox == w_fox // HEAD_DIM
    assert n_heads_fox <= SUBLANES and d_ff % TF == 0 and D % LANES == 0
    assert conv_ch % LANES == 0 and S % TM_INPROJ == 0 and S % TQ == 0 and S % TS_OUT == 0 and S % TS_FFN == 0
    assert CONV_K - 1 <= CONV_HALO and TS_OUT % CONV_ROWS == 0 and CONV_ROWS % CONV_HALO == 0
    n_att = w_fox // LANES
    off_fa = 3 * w_fox
    off_qb = off_fa + n_heads_fox
    off_gv = off_qb + 3 * w_dil
    assert w_in.shape[-1] == off_gv + 2 * conv_ch

    logw = _dil_log_weights(S, TQ)
    row = lambda v: v.reshape(1, -1)

    for l in range(depth):
        w_f = jnp.pad(w_in[l, :, off_fa:off_qb], ((0, 0), (0, LANES - n_heads_fox)))
        w_a = jnp.concatenate([w_in[l, :, :off_fa], w_f], axis=1).astype(BF16)
        w_b = w_in[l, :, off_qb:].astype(BF16)
        b_f = jnp.pad(b_forget[l], (0, LANES - n_heads_fox)).reshape(1, LANES)

        qkv, gl, f = _inproj(x, row(ln1_g[l]), w_a, w_b, n_att=n_att, conv_ch=conv_ch)
        ccol, crow = _cumsum(f, b_f)
        oa = _fox_attention(qkv, ccol, crow, n_att=n_att)
        ob = _dil_attention(qkv, logw, n_att=n_att)
        x = _outproj(x, oa, ob, gl, row(g_out_fox[l]), row(g_out_dil[l]), conv_w[l], row(conv_b[l]),
                     row(cnorm_g[l]), row(cnorm_b[l]), w_o[l].astype(BF16))
        x = _ffn(x, row(ln2_g[l]), w_up[l].astype(BF16), ffn_conv_w[l], row(ffn_conv_b[l]),
                 w_down[l].astype(BF16), row(g_final), final=(l == depth - 1))
    return x
```

```python
import functools

import numpy as np
import jax
import jax.numpy as jnp
from jax import lax
from jax.experimental import pallas as pl
from jax.experimental.pallas import tpu as pltpu

F32 = jnp.float32
BF16 = jnp.bfloat16

HEAD_DIM = 64
LANES = 128
SUBLANES = 8
DILATION_PAIRS = ((128, 1), (512, 4), (2048, 16))
CONV_K = 31
FFN_CONV_K = 3
EPS = 1e-6
LOG2E = 1.4426950408889634
VMEM_LIMIT = 56 * 1024 * 1024

TM_INPROJ = 1024
INPROJ_COLS = 512
TQ = 512
TS_OUT = 512
CONV_HALO = 32
CONV_ROWS = 64
TS_FFN = 1024
FFN_HALO = 16
FFN_ROWS = 1024
TF = 256


def _params(*sem):
    return pltpu.CompilerParams(dimension_semantics=sem, vmem_limit_bytes=VMEM_LIMIT)


def _sigmoid(x):
    return 1.0 / (1.0 + jnp.exp(-x))


def _inproj_kernel(x_ref, g_ref, wa_ref, wb_ref, qkv_ref, gl_ref, f_ref, h_scr, *, n_att, conv_ch):
    x = x_ref[...]
    ms = jnp.mean(x * x, axis=-1, keepdims=True)
    h_scr[...] = (x * lax.rsqrt(ms + EPS) * g_ref[...]).astype(BF16)
    h = h_scr[...]
    n_qkv = 3 * n_att
    n_glu = conv_ch // LANES
    glu = {}

    def emit(group, slab, r):
        if slab < n_qkv:
            if slab < n_att:
                r = r * (HEAD_DIM ** -0.5 * LOG2E)
            qkv_ref[group * n_qkv + slab] = r.astype(BF16)
        elif group == 0:
            f_ref[...] = r
        else:
            glu[slab - n_qkv] = r
            for j in range(n_glu):
                if j in glu and j + n_glu in glu:
                    gl_ref[:, j * LANES:(j + 1) * LANES] = (glu.pop(j) * _sigmoid(glu.pop(j + n_glu))).astype(BF16)

    for group, w_ref in enumerate((wa_ref, wb_ref)):
        n_cols = w_ref.shape[1]
        for c0 in range(0, n_cols, INPROJ_COLS):
            width = min(INPROJ_COLS, n_cols - c0)
            r = jnp.dot(h, w_ref[:, c0:c0 + width], preferred_element_type=F32)
            for j in range(width // LANES):
                emit(group, c0 // LANES + j, r[:, j * LANES:(j + 1) * LANES])


def _inproj(x, g, wa, wb, *, n_att, conv_ch):
    B, S, D = x.shape
    tm = TM_INPROJ
    kern = functools.partial(_inproj_kernel, n_att=n_att, conv_ch=conv_ch)
    return pl.pallas_call(
        kern,
        grid=(B, S // tm),
        in_specs=[
            pl.BlockSpec((None, tm, D), lambda b, i: (b, i, 0)),
            pl.BlockSpec((1, D), lambda b, i: (0, 0)),
            pl.BlockSpec(wa.shape, lambda b, i: (0, 0)),
            pl.BlockSpec(wb.shape, lambda b, i: (0, 0)),
        ],
        out_specs=[
            pl.BlockSpec((None, 6 * n_att, tm, LANES), lambda b, i: (b, 0, i, 0)),
            pl.BlockSpec((None, tm, conv_ch), lambda b, i: (b, i, 0)),
            pl.BlockSpec((None, tm, LANES), lambda b, i: (b, i, 0)),
        ],
        out_shape=[
            jax.ShapeDtypeStruct((B, 6 * n_att, S, LANES), BF16),
            jax.ShapeDtypeStruct((B, S, conv_ch), BF16),
            jax.ShapeDtypeStruct((B, S, LANES), F32),
        ],
        scratch_shapes=[pltpu.VMEM((tm, D), BF16)],
        compiler_params=_params("parallel", "parallel"),
        name="inproj",
    )(x, g, wa, wb)


def _cumsum_kernel(f_ref, b_ref, ccol_ref, crow_ref):
    S = f_ref.shape[0]
    ri = lax.broadcasted_iota(jnp.int32, (LANES, LANES), 0)
    ci = lax.broadcasted_iota(jnp.int32, (LANES, LANES), 1)
    tri = jnp.where(ci <= ri, 1.0, 0.0).astype(BF16)
    carry = jnp.zeros((1, LANES), F32)
    for blk in range(S // LANES):
        rows = slice(blk * LANES, (blk + 1) * LANES)
        z = f_ref[rows, :] + b_ref[...]
        a = jnp.minimum(z, 0.0) - jnp.log(1.0 + jnp.exp(-jnp.abs(z)))
        hi = a.astype(BF16)
        r1 = a - hi.astype(F32)
        mid = r1.astype(BF16)
        lo = (r1 - mid.astype(F32)).astype(BF16)
        c = (jnp.dot(tri, hi, preferred_element_type=F32)
             + jnp.dot(tri, mid, preferred_element_type=F32)
             + jnp.dot(tri, lo, preferred_element_type=F32)) + carry
        ccol_ref[rows, :] = c * LOG2E
        crow_ref[:, rows] = jnp.transpose(c * LOG2E)[0:SUBLANES, :]
        carry = c[LANES - 1:LANES, :]


def _cumsum(f, bias):
    B, S, _ = f.shape
    return pl.pallas_call(
        _cumsum_kernel,
        grid=(B,),
        in_specs=[
            pl.BlockSpec((None, S, LANES), lambda b: (b, 0, 0)),
            pl.BlockSpec((1, LANES), lambda b: (0, 0)),
        ],
        out_specs=[
            pl.BlockSpec((None, S, LANES), lambda b: (b, 0, 0)),
            pl.BlockSpec((None, SUBLANES, S), lambda b: (b, 0, 0)),
        ],
        out_shape=[
            jax.ShapeDtypeStruct((B, S, LANES), F32),
            jax.ShapeDtypeStruct((B, SUBLANES, S), F32),
        ],
        compiler_params=_params("parallel"),
        name="cumsum",
    )(f, bias)


_NT = (((1,), (1,)), ((), ()))


def _run_blocks(n, block_fn):
    def quad(j, carry):
        for u in range(4):
            block_fn(4 * j + u)
        return carry

    lax.fori_loop(0, lax.shift_right_logical(n, 2), quad, 0)
    done = jnp.bitwise_and(n, -4)

    @pl.when(jnp.bitwise_and(n, 2) == 2)
    def _():
        block_fn(done)
        block_fn(done + 1)

    @pl.when(jnp.bitwise_and(n, 1) == 1)
    def _():
        block_fn(n - 1)


def _build_value_operands(v_ref, vone_scr, qi, tq, n_pairs):
    lane = lax.broadcasted_iota(jnp.int32, (tq, LANES), 1)

    @pl.when(qi == 0)
    def _():
        def build(r, carry):
            r0 = pl.multiple_of(r * tq, tq)
            for pr in range(n_pairs):
                vf = v_ref[pr, pl.ds(r0, tq), :].astype(F32)
                vone_scr[2 * pr, pl.ds(r0, tq), :] = jnp.where(lane < HEAD_DIM, vf, 1.0).astype(BF16)
                vone_scr[2 * pr + 1, pl.ds(r0, tq), :] = jnp.where(lane >= HEAD_DIM, vf, 1.0).astype(BF16)
            return carry

        lax.fori_loop(0, v_ref.shape[1] // tq, build, 0)


def _attn_two_pass(k_block, o_ref, scr, qi, bias_fn, *, tq, n_pairs):
    qm_scr, vone_scr, s_scr, mx_scr, acc_scr = scr
    halves = tq // LANES
    lane = lax.broadcasted_iota(jnp.int32, (tq, LANES), 1)

    def score_block(kb, diag):
        k0 = pl.multiple_of(kb * tq, tq)
        for h in range(2 * n_pairs):
            s = lax.dot_general(qm_scr[h], k_block(h, k0), _NT, preferred_element_type=F32)
            cols = [bias_fn(h, c, s[:, c * LANES:(c + 1) * LANES], kb, diag) for c in range(halves)]
            s_scr[h, kb] = jnp.concatenate(cols, axis=1)
            blkmax = functools.reduce(jnp.maximum, cols)
            mx_scr[h] = blkmax if diag else jnp.maximum(mx_scr[h], blkmax)

    score_block(qi, True)
    _run_blocks(qi, lambda kb: score_block(kb, False))

    for h in range(2 * n_pairs):
        m = jnp.max(mx_scr[h], axis=-1, keepdims=True)
        mx_scr[h] = jnp.broadcast_to(m, (tq, LANES))
        acc_scr[h] = jnp.zeros((tq, LANES), F32)

    def value_block(kb):
        k0 = pl.multiple_of(kb * tq, tq)
        for h in range(2 * n_pairs):
            m = mx_scr[h]
            s = s_scr[h, kb]
            p = [jnp.exp2(s[:, c * LANES:(c + 1) * LANES] - m) for c in range(halves)]
            acc_scr[h] += jnp.dot(jnp.concatenate(p, axis=1).astype(BF16), vone_scr[h, pl.ds(k0, tq), :],
                                  preferred_element_type=F32)

    _run_blocks(qi + 1, value_block)

    for pr in range(n_pairs):
        a0, a1 = acc_scr[2 * pr], acc_scr[2 * pr + 1]
        num = jnp.where(lane < HEAD_DIM, a0, a1)
        den = pltpu.roll(jnp.where(lane < HEAD_DIM, a1, a0), HEAD_DIM, axis=1)
        o_ref[:, pr * LANES:(pr + 1) * LANES] = (num / den).astype(o_ref.dtype)


def _causal_mask(c, s, tq):
    ri = lax.broadcasted_iota(jnp.int32, (tq, LANES), 0)
    ci = lax.broadcasted_iota(jnp.int32, (tq, LANES), 1) + c * LANES
    return jnp.where(ci <= ri, s, -jnp.inf)


def _mask_head_queries(q_ref, qm_scr, tq, n_pairs):
    lane = lax.broadcasted_iota(jnp.int32, (tq, LANES), 1)
    for pr in range(n_pairs):
        qf = q_ref[pr].astype(F32)
        qm_scr[2 * pr] = jnp.where(lane < HEAD_DIM, qf, 0.0).astype(BF16)
        qm_scr[2 * pr + 1] = jnp.where(lane >= HEAD_DIM, qf, 0.0).astype(BF16)


def _fox_kernel(q_ref, k_ref, v_ref, ccol_ref, crow_ref, o_ref, cq_scr, *scr, tq, n_pairs):
    qi = pl.program_id(1)
    _build_value_operands(v_ref, scr[1], qi, tq, n_pairs)
    _mask_head_queries(q_ref, scr[0], tq, n_pairs)
    ccol_blk = ccol_ref[...]
    for h in range(2 * n_pairs):
        cq_scr[h] = jnp.broadcast_to(ccol_blk[:, h:h + 1], (tq, LANES))

    def bias_fn(h, c, s, kb, diag):
        k0 = pl.multiple_of(kb * tq, tq)
        c_k = crow_ref[h:h + 1, pl.ds(k0 + c * LANES, LANES)]
        s = (s + cq_scr[h]) - c_k
        return _causal_mask(c, s, tq) if diag else s

    k_block = lambda h, k0: k_ref[h // 2, pl.ds(k0, tq), :]
    _attn_two_pass(k_block, o_ref, scr, qi, bias_fn, tq=tq, n_pairs=n_pairs)


def _dil_kernel(q_ref, k_ref, v_ref, lw_ref, o_ref, *scr, tq, n_pairs):
    qi = pl.program_id(1)
    _build_value_operands(v_ref, scr[1], qi, tq, n_pairs)
    _mask_head_queries(q_ref, scr[0], tq, n_pairs)

    def bias_fn(h, c, s, kb, diag):
        off = 0 if diag else qi - kb
        return s + lw_ref[off, :, c * LANES:(c + 1) * LANES]

    k_block = lambda h, k0: k_ref[h // 2, pl.ds(k0, tq), :]
    _attn_two_pass(k_block, o_ref, scr, qi, bias_fn, tq=tq, n_pairs=n_pairs)


def _qkv_specs(S, tq, n_att, group):
    base = 3 * group
    return [
        pl.BlockSpec((None, n_att, tq, LANES), lambda b, qi: (b, base, qi, 0)),
        pl.BlockSpec((None, n_att, S, LANES), lambda b, qi: (b, base + 1, 0, 0)),
        pl.BlockSpec((None, n_att, S, LANES), lambda b, qi: (b, base + 2, 0, 0)),
    ]


def _attn_scratch(S, tq, n_att):
    n_heads = 2 * n_att
    head_tile = pltpu.VMEM((n_heads, tq, LANES), F32)
    return [
        pltpu.VMEM((n_heads, tq, LANES), BF16),
        pltpu.VMEM((n_heads, S, LANES), BF16),
        pltpu.VMEM((n_heads, S // tq, tq, tq), F32),
        head_tile, head_tile,
    ]


def _fox_attention(qkv, ccol, crow, *, n_att):
    B, _, S, _ = qkv.shape
    tq = TQ
    return pl.pallas_call(
        functools.partial(_fox_kernel, tq=tq, n_pairs=n_att),
        grid=(B, S // tq),
        in_specs=_qkv_specs(S, tq, n_att, 0) + [
            pl.BlockSpec((None, tq, LANES), lambda b, qi: (b, qi, 0)),
            pl.BlockSpec((None, SUBLANES, S), lambda b, qi: (b, 0, 0)),
        ],
        out_specs=pl.BlockSpec((None, tq, n_att * LANES), lambda b, qi: (b, qi, 0)),
        out_shape=jax.ShapeDtypeStruct((B, S, n_att * LANES), BF16),
        scratch_shapes=[pltpu.VMEM((2 * n_att, tq, LANES), F32)] + _attn_scratch(S, tq, n_att),
        compiler_params=_params("arbitrary", "arbitrary"),
        name="fox_attn",
    )(qkv, qkv, qkv, ccol, crow)


def _dil_log_weights(S, tq):
    for window, d in DILATION_PAIRS:
        assert S % ((window // d) * d) == 0, "dilated branches assume no sequence padding"
    nq = S // tq
    r = np.arange(tq)[:, None]
    c = np.arange(tq)[None, :]
    tabs = []
    for off in range(nq):
        delta = off * tq + r - c
        w = np.zeros((tq, tq), np.int64)
        for window, d in DILATION_PAIRS:
            w += (delta >= 0) & (delta % d == 0) & (delta // d <= window // d)
        with np.errstate(divide="ignore"):
            tabs.append(np.log2(w.astype(np.float64)))
    return jnp.asarray(np.stack(tabs).astype(np.float32))


def _dil_attention(qkv, logw, *, n_att):
    B, _, S, _ = qkv.shape
    tq = TQ
    nq = S // tq
    return pl.pallas_call(
        functools.partial(_dil_kernel, tq=tq, n_pairs=n_att),
        grid=(B, nq),
        in_specs=_qkv_specs(S, tq, n_att, 1) + [
            pl.BlockSpec((nq, tq, tq), lambda b, qi: (0, 0, 0)),
        ],
        out_specs=pl.BlockSpec((None, tq, n_att * LANES), lambda b, qi: (b, qi, 0)),
        out_shape=jax.ShapeDtypeStruct((B, S, n_att * LANES), BF16),
        scratch_shapes=_attn_scratch(S, tq, n_att),
        compiler_params=_params("arbitrary", "arbitrary"),
        name="dil_attn",
    )(qkv, qkv, qkv, logw)


def _shift_matrices():
    k = CONV_ROWS + CONV_HALO
    s = np.zeros((SUBLANES, k, k), np.float32)
    for r in range(SUBLANES):
        s[r, np.arange(k - r), np.arange(k - r) + r] = 1.0
    return jnp.asarray(s, dtype=BF16)


def _outproj_kernel(x_ref, oa_ref, ob_ref, gl_ref, glh_ref, sh_ref, gfox_ref, gdil_ref, cw_ref, cb_ref,
                    cng_ref, cnb_ref, wo_ref, o_ref, ext_scr, y_scr, *, ts, w_att, conv_ch):
    i = pl.program_id(1)
    ext_scr[0:CONV_HALO, :] = jnp.where(i > 0, glh_ref[...], jnp.zeros_like(glh_ref))
    ext_scr[CONV_HALO:, :] = gl_ref[...]
    first = CONV_HALO - (CONV_K - 1)
    for c in range(ts // CONV_ROWS):
        r0 = c * CONV_ROWS
        win = ext_scr[r0:r0 + CONV_ROWS + CONV_HALO, :]
        acc = jnp.broadcast_to(cb_ref[...], (CONV_ROWS, conv_ch))
        for r in range(SUBLANES):
            shifted = jnp.dot(sh_ref[r], win, preferred_element_type=F32)
            for k in range(CONV_K):
                e = first + k
                if e % SUBLANES == r:
                    a0 = e - r
                    acc = acc + cw_ref[k:k + 1, :] * shifted[a0:a0 + CONV_ROWS, :]
        mu = jnp.mean(acc, axis=-1, keepdims=True)
        xc = acc - mu
        yn = xc * lax.rsqrt(jnp.mean(xc * xc, axis=-1, keepdims=True) + EPS)
        yn = yn * cng_ref[...] + cnb_ref[...]
        y_scr[r0:r0 + CONV_ROWS, 2 * w_att:] = (yn * _sigmoid(yn)).astype(BF16)
    for o_in, g_in, lo in ((oa_ref, gfox_ref, 0), (ob_ref, gdil_ref, w_att)):
        a = o_in[...].astype(F32)
        ms = jnp.mean(a * a, axis=-1, keepdims=True)
        y_scr[:, lo:lo + w_att] = (a * lax.rsqrt(ms + EPS) * g_in[...]).astype(BF16)
    o_ref[...] = x_ref[...] + jnp.dot(y_scr[...], wo_ref[...], preferred_element_type=F32)


def _outproj(x, oa, ob, gl, g_fox, g_dil, cw, cb, cng, cnb, wo):
    B, S, D = x.shape
    ts = TS_OUT
    w_att = oa.shape[-1]
    conv_ch = gl.shape[-1]
    hb = ts // CONV_HALO
    shifts = _shift_matrices()
    kern = functools.partial(_outproj_kernel, ts=ts, w_att=w_att, conv_ch=conv_ch)
    vec = lambda n: pl.BlockSpec((1, n), lambda b, i: (0, 0))
    return pl.pallas_call(
        kern,
        grid=(B, S // ts),
        in_specs=[
            pl.BlockSpec((None, ts, D), lambda b, i: (b, i, 0)),
            pl.BlockSpec((None, ts, w_att), lambda b, i: (b, i, 0)),
            pl.BlockSpec((None, ts, w_att), lambda b, i: (b, i, 0)),
            pl.BlockSpec((None, ts, conv_ch), lambda b, i: (b, i, 0)),
            pl.BlockSpec((None, CONV_HALO, conv_ch), lambda b, i: (b, jnp.maximum(i * hb - 1, 0), 0)),
            pl.BlockSpec(shifts.shape, lambda b, i: (0, 0, 0)),
            vec(w_att), vec(w_att),
            pl.BlockSpec((CONV_K, conv_ch), lambda b, i: (0, 0)),
            vec(conv_ch), vec(conv_ch), vec(conv_ch),
            pl.BlockSpec((D, D), lambda b, i: (0, 0)),
        ],
        out_specs=pl.BlockSpec((None, ts, D), lambda b, i: (b, i, 0)),
        out_shape=jax.ShapeDtypeStruct((B, S, D), F32),
        scratch_shapes=[pltpu.VMEM((ts + CONV_HALO, conv_ch), BF16), pltpu.VMEM((ts, D), BF16)],
        compiler_params=_params("parallel", "parallel"),
        name="outproj",
    )(x, oa, ob, gl, gl, shifts, g_fox, g_dil, cw, cb, cng, cnb, wo)


def _ffn_kernel(x_ref, xh_ref, g_ref, wup_ref, cw_ref, cb_ref, wdn_ref, gf_ref, o_ref,
                h_scr, hid_scr, *, ts, d_ff, final):
    i = pl.program_id(1)
    n_chunks = d_ff // TF

    def norm(v):
        return (v * lax.rsqrt(jnp.mean(v * v, axis=-1, keepdims=True) + EPS) * g_ref[...]).astype(BF16)

    h_scr[0:FFN_HALO, :] = norm(jnp.where(i > 0, xh_ref[...], 0.0))
    h_scr[FFN_HALO:, :] = norm(x_ref[...])
    first = FFN_HALO - (FFN_CONV_K - 1)

    def cols(c, part):
        return pl.ds(pl.multiple_of(part * d_ff + c * TF, TF), TF)

    def chunk(c):
        for r0 in range(0, ts, FFN_ROWS):
            hb = h_scr[r0:r0 + FFN_ROWS + FFN_HALO, :]
            halves = []
            for part in range(2):
                u = jnp.dot(hb, wup_ref[:, cols(c, part)], preferred_element_type=F32)
                w = cw_ref[:, cols(c, part)]
                uc = cb_ref[:, cols(c, part)]
                for k in range(FFN_CONV_K):
                    uc = uc + w[k:k + 1, :] * u[first + k:first + k + FFN_ROWS, :]
                halves.append(uc)
            gate, val = halves
            hid_scr[c, r0:r0 + FFN_ROWS, :] = (gate * _sigmoid(gate) * val).astype(BF16)

    def pair(j, carry):
        chunk(2 * j)
        chunk(2 * j + 1)
        return carry

    lax.fori_loop(0, n_chunks // 2, pair, 0)
    if n_chunks % 2:
        chunk(n_chunks - 1)

    acc = jnp.dot(hid_scr[0], wdn_ref[0:TF, :], preferred_element_type=F32)
    for c in range(1, n_chunks):
        acc = acc + jnp.dot(hid_scr[c], wdn_ref[c * TF:(c + 1) * TF, :], preferred_element_type=F32)
    out = x_ref[...] + acc
    if final:
        out = out * lax.rsqrt(jnp.mean(out * out, axis=-1, keepdims=True) + EPS) * gf_ref[...]
    o_ref[...] = out


def _ffn(x, g, wup, cw, cb, wdn, g_final, *, final):
    B, S, D = x.shape
    ts = TS_FFN
    d_ff = wdn.shape[0]
    hb = ts // FFN_HALO
    kern = functools.partial(_ffn_kernel, ts=ts, d_ff=d_ff, final=final)
    const = lambda shape: pl.BlockSpec(shape, lambda b, i: (0,) * len(shape), pipeline_mode=pl.Buffered(1))
    return pl.pallas_call(
        kern,
        grid=(B, S // ts),
        in_specs=[
            pl.BlockSpec((None, ts, D), lambda b, i: (b, i, 0)),
            pl.BlockSpec((None, FFN_HALO, D), lambda b, i: (b, jnp.maximum(i * hb - 1, 0), 0)),
            const((1, D)),
            const((D, 2 * d_ff)),
            const((FFN_CONV_K, 2 * d_ff)),
            const((1, 2 * d_ff)),
            const((d_ff, D)),
            const((1, D)),
        ],
        out_specs=pl.BlockSpec((None, ts, D), lambda b, i: (b, i, 0)),
        out_shape=jax.ShapeDtypeStruct((B, S, D), F32),
        scratch_shapes=[
            pltpu.VMEM((ts + FFN_HALO, D), BF16),
            pltpu.VMEM((d_ff // TF, ts, TF), BF16),
        ],
        compiler_params=_params("parallel", "parallel"),
        name="ffn_final" if final else "ffn",
    )(x, x, g, wup, cw, cb, wdn, g_final)


def kernel(x, ln1_g, w_in, b_forget, g_out_fox, g_out_dil, conv_w, conv_b, cnorm_g, cnorm_b,
           w_o, ln2_g, w_up, ffn_conv_w, ffn_conv_b, w_down, g_final):
    B, S, D = x.shape
    depth = w_in.shape[0]
    w_fox = g_out_fox.shape[-1]
    w_dil = g_out_dil.shape[-1]
    conv_ch = conv_w.shape[-1]
    n_heads_fox = b_forget.shape[-1]
    d_ff = w_down.shape[1]
    assert w_fox == w_dil and w_fox % LANES == 0 and n_heads_fox == w_fox // HEAD_DIM
    assert n_heads_fox <= SUBLANES and d_ff % TF == 0 and D % LANES == 0
    assert conv_ch % LANES == 0 and S % TM_INPROJ == 0 and S % TQ == 0 and S % TS_OUT == 0 and S % TS_FFN == 0
    assert CONV_K - 1 <= CONV_HALO and TS_OUT % CONV_ROWS == 0 and CONV_ROWS % CONV_HALO == 0
    n_att = w_fox // LANES
    off_fa = 3 * w_fox
    off_qb = off_fa + n_heads_fox
    off_gv = off_qb + 3 * w_dil
    assert w_in.shape[-1] == off_gv + 2 * conv_ch

    logw = _dil_log_weights(S, TQ)
    row = lambda v: v.reshape(1, -1)

    for l in range(depth):
        w_f = jnp.pad(w_in[l, :, off_fa:off_qb], ((0, 0), (0, LANES - n_heads_fox)))
        w_a = jnp.concatenate([w_in[l, :, :off_fa], w_f], axis=1).astype(BF16)
        w_b = w_in[l, :, off_qb:].astype(BF16)
        b_f = jnp.pad(b_forget[l], (0, LANES - n_heads_fox)).reshape(1, LANES)

        qkv, gl, f = _inproj(x, row(ln1_g[l]), w_a, w_b, n_att=n_att, conv_ch=conv_ch)
        ccol, crow = _cumsum(f, b_f)
        oa = _fox_attention(qkv, ccol, crow, n_att=n_att)
        ob = _dil_attention(qkv, logw, n_att=n_att)
        x = _outproj(x, oa, ob, gl, row(g_out_fox[l]), row(g_out_dil[l]), conv_w[l], row(conv_b[l]),
                     row(cnorm_g[l]), row(cnorm_b[l]), w_o[l].astype(BF16))
        x = _ffn(x, row(ln2_g[l]), w_up[l].astype(BF16), ffn_conv_w[l], row(ffn_conv_b[l]),
                 w_down[l].astype(BF16), row(g_final), final=(l == depth - 1))
    return x
```
